```python
import jax, jax.numpy as jnp
from jax import lax
import numpy as np

D_MODEL = 2048
BATCH = 2
SEQ = 4096
DEPTH = 4
DEC_BATCH = 128
DEC_SEQ = 4
PAST_LEN = 8192
PAGE_SIZE = 128

N_BRANCH = 4
BRANCH_W = D_MODEL // N_BRANCH
SB_HEAD_DIM = 64
SB_HEADS = BRANCH_W // SB_HEAD_DIM
SB_KV_HEADS = 2
SB_GROUP = SB_HEADS // SB_KV_HEADS
CONV_CH = BRANCH_W
CONV_WIDTH = 31
MLA_V_DIM = 64
MLA_HEADS = BRANCH_W // MLA_V_DIM
MLA_NOPE_DIM = 64
MLA_ROPE_DIM = 32
MLA_Q_LORA = D_MODEL // 4
MLA_KV_LORA = D_MODEL // 8
MLA_SCALE = (MLA_NOPE_DIM + MLA_ROPE_DIM) ** -0.5
ROPE_THETA = 10000.0
POOL_CH = BRANCH_W
POOL_WINDOWS = (2, 4, 8, 16)
POOL_GROUP = POOL_CH // len(POOL_WINDOWS)
POOL_STATE = max(POOL_WINDOWS) - 1
N_GROUPS = 4
EXPERTS_PER_GROUP = 8
N_EXPERTS = N_GROUPS * EXPERTS_PER_GROUP
TOP_K = 2
EXPERT_FF = 256
Q_BLOCK = 128
LN_EPS = 1e-5
RMS_EPS = 1e-6
DEEPNORM_ALPHA = (2 * DEPTH) ** 0.25
DEEPNORM_BETA = (8 * DEPTH) ** -0.25
IN_SPLITS = (SB_HEADS * SB_HEAD_DIM, SB_KV_HEADS * SB_HEAD_DIM, SB_KV_HEADS * SB_HEAD_DIM, 2 * CONV_CH,
             MLA_Q_LORA, MLA_KV_LORA, MLA_ROPE_DIM, POOL_CH, N_BRANCH * D_MODEL)
N_IN = sum(IN_SPLITS)

kernel_name = 'hybrid_sb_conv_mla_pool_hmoe_step'

F32 = jnp.float32


def layer_norm(x):
    xf = x.astype(F32)
    mu = xf.mean(-1, keepdims=True)
    var = jnp.square(xf - mu).mean(-1, keepdims=True)
    return ((xf - mu) * lax.rsqrt(var + LN_EPS)).astype(x.dtype)


def rms_norm(x, g):
    xf = x.astype(F32)
    return (xf * lax.rsqrt(jnp.mean(xf * xf, -1, keepdims=True) + RMS_EPS)).astype(x.dtype) * g


def rope(x, pos):
    half = x.shape[-1] // 2
    inv_freq = ROPE_THETA ** (-jnp.arange(half, dtype=F32) / half)
    ang = pos.astype(F32)[:, None] * inv_freq
    ang = ang.reshape((pos.shape[0],) + (1,) * (x.ndim - 3) + (half,))
    cos, sin = jnp.cos(ang), jnp.sin(ang)
    xf = x.astype(F32)
    x1, x2 = xf[..., :half], xf[..., half:]
    return jnp.concatenate([x1 * cos - x2 * sin, x1 * sin + x2 * cos], -1).astype(x.dtype)


def sweep_query_blocks(fn, qs, qpos):
    T = qpos.shape[0]
    if T <= Q_BLOCK or T % Q_BLOCK:
        return fn(qs, qpos)
    nb = T // Q_BLOCK
    B = qs[0].shape[0]
    blocks = tuple(jnp.moveaxis(q.reshape((B, nb, Q_BLOCK) + q.shape[2:]), 1, 0) for q in qs)
    out = lax.map(lambda a: fn(a[0], a[1]), (blocks, qpos.reshape(nb, Q_BLOCK)))
    return jnp.moveaxis(out, 0, 1).reshape(B, T, out.shape[-1])


def stick_breaking_block(q, qpos, k, v, kpos):
    z = jnp.einsum('bthgd,bshd->bhgts', q, k, preferred_element_type=F32) * (SB_HEAD_DIM ** -0.5)
    mask = kpos[None, :] < qpos[:, None]
    log_keep = jnp.where(mask, jax.nn.log_sigmoid(-z), 0.0)
    log_tail = lax.cumsum(log_keep, axis=z.ndim - 1, reverse=True) - log_keep
    a = jnp.where(mask, jnp.exp(jax.nn.log_sigmoid(z) + log_tail), 0.0)
    o = jnp.einsum('bhgts,bshd->bthgd', a, v.astype(F32))
    return o.reshape(o.shape[0], o.shape[1], -1).astype(v.dtype)


def mla_block(q_lat, q_rope, qpos, lat, krope, kpos, w_uv):
    s = (jnp.einsum('bthr,bsr->bhts', q_lat, lat, preferred_element_type=F32)
         + jnp.einsum('bthd,bsd->bhts', q_rope, krope, preferred_element_type=F32)) * MLA_SCALE
    s = jnp.where(kpos[None, :] <= qpos[:, None], s, -1e30)
    p = jax.nn.softmax(s, axis=-1)
    o_lat = jnp.einsum('bhts,bsr->bthr', p, lat.astype(F32))
    o = jnp.einsum('bthr,rhe->bthe', o_lat, w_uv.astype(F32))
    return o.reshape(o.shape[0], o.shape[1], -1).astype(lat.dtype)


def conformer_conv(glu_in, prev, w_dw, b_dw, ln_g, ln_b):
    u = glu_in[..., :CONV_CH] * jax.nn.sigmoid(glu_in[..., CONV_CH:])
    ext = jnp.concatenate([prev.astype(u.dtype), u], axis=1)
    y = lax.conv_general_dilated(ext, w_dw[:, None, :].astype(ext.dtype), (1,), 'VALID',
                                 dimension_numbers=('NWC', 'WIO', 'NWC'),
                                 feature_group_count=CONV_CH) + b_dw
    y = layer_norm(y) * ln_g + ln_b
    return jax.nn.silu(y), ext[:, -(CONV_WIDTH - 1):]


def pool_mixer(u, prev, pos0, w_grp, ch_scale):
    B, T, C = u.shape
    n_prev = prev.shape[1]
    ext = jnp.concatenate([prev.astype(u.dtype), u], axis=1)
    cs = jnp.pad(jnp.cumsum(ext.astype(F32), axis=1), ((0, 0), (1, 0), (0, 0)))
    pos = pos0 + jnp.arange(T)
    end = cs[:, n_prev + 1:n_prev + 1 + T]
    pooled = []
    for g, w in enumerate(POOL_WINDOWS):
        sl = slice(g * POOL_GROUP, (g + 1) * POOL_GROUP)
        start = cs[:, n_prev + 1 - w:n_prev + 1 - w + T, sl]
        cnt = jnp.minimum(pos + 1, w).astype(F32)[None, :, None]
        pooled.append((end[..., sl] - start) / cnt)
    pooled = jnp.stack(pooled, axis=2)
    diff = pooled - u.reshape(B, T, len(POOL_WINDOWS), POOL_GROUP).astype(F32)
    mixed = jnp.einsum('btgc,gcd->btgd', diff, w_grp.astype(F32)).reshape(B, T, C) * ch_scale
    return mixed.astype(u.dtype), ext[:, -n_prev:]


def token_mixer(h, past, lw):
    sb_k_past, sb_v_past, lat_past, krope_past, conv_prev, pool_prev = past
    B, T, _ = h.shape
    P = sb_k_past.shape[1]
    qpos = P + jnp.arange(T)
    kpos = jnp.arange(P + T)
    z = jnp.einsum('btd,dn->btn', h, lw['w_in'])
    offs = np.cumsum(IN_SPLITS)[:-1].tolist()
    sb_q, sb_k, sb_v, glu_in, c_q, c_kv, k_rope, pool_u, gate_logits = jnp.split(z, offs, axis=-1)
    q = sb_q.reshape(B, T, SB_KV_HEADS, SB_GROUP, SB_HEAD_DIM)
    k_new = sb_k.reshape(B, T, SB_KV_HEADS, SB_HEAD_DIM)
    v_new = sb_v.reshape(B, T, SB_KV_HEADS, SB_HEAD_DIM)
    k_all = jnp.concatenate([sb_k_past.astype(h.dtype), k_new], axis=1)
    v_all = jnp.concatenate([sb_v_past.astype(h.dtype), v_new], axis=1)
    o_a = sweep_query_blocks(lambda qs, qp: stick_breaking_block(qs[0], qp, k_all, v_all, kpos), (q,), qpos)
    o_b, conv_new = conformer_conv(glu_in, conv_prev, lw['conv_w'], lw['conv_b'], lw['conv_ln_g'], lw['conv_ln_b'])
    cq = rms_norm(c_q, lw['mla_q_norm'])
    qh = jnp.einsum('btq,qhe->bthe', cq, lw['w_uq'])
    q_nope, q_rope = qh[..., :MLA_NOPE_DIM], rope(qh[..., MLA_NOPE_DIM:], qpos)
    q_lat = jnp.einsum('bthd,rhd->bthr', q_nope, lw['w_uk'])
    lat_new = rms_norm(c_kv, lw['mla_kv_norm'])
    krope_new = rope(k_rope, qpos)
    lat_all = jnp.concatenate([lat_past.astype(h.dtype), lat_new], axis=1)
    krope_all = jnp.concatenate([krope_past.astype(h.dtype), krope_new], axis=1)
    o_c = sweep_query_blocks(lambda qs, qp: mla_block(qs[0], qs[1], qp, lat_all, krope_all, kpos, lw['w_uv']),
                             (q_lat, q_rope), qpos)
    o_d, pool_new = pool_mixer(pool_u, pool_prev, P, lw['pool_w'], lw['pool_scale'])
    branches = jnp.stack([o_a, o_b, o_c, o_d], axis=2)
    proj = jnp.einsum('btic,icd->btid', branches, lw['w_branch'])
    gates = jax.nn.sigmoid(gate_logits.reshape(B, T, N_BRANCH, D_MODEL))
    merged = jnp.sum(gates * proj, axis=2)
    out = jnp.einsum('btd,de->bte', merged, lw['w_out'])
    return out, (k_new, v_new, lat_new, krope_new, conv_new, pool_new)


def hier_moe(h, lw):
    grp_logits = jnp.einsum('btd,dg->btg', h, lw['w_router_grp'], preferred_element_type=F32) + lw['b_router_grp']
    grp_w, grp_idx = lax.top_k(jax.nn.softmax(grp_logits, axis=-1), 1)
    exp_logits = jnp.einsum('btd,dge->btge', h, lw['w_router_exp'], preferred_element_type=F32) + lw['b_router_exp']
    exp_logits = jnp.take_along_axis(exp_logits, grp_idx[..., None], axis=2)[:, :, 0]
    top_logits, top_idx = lax.top_k(exp_logits, TOP_K)
    weights = grp_w * jax.nn.softmax(top_logits, axis=-1)
    expert_id = grp_idx * EXPERTS_PER_GROUP + top_idx
    combine = jnp.sum(jax.nn.one_hot(expert_id, N_EXPERTS, dtype=F32) * weights[..., None], axis=2)
    a = jnp.einsum('btd,edf->btef', h, lw['w_exp_gate'])
    u = jnp.einsum('btd,edf->btef', h, lw['w_exp_up'])
    hid = jax.nn.silu(a) * u * combine[..., None].astype(h.dtype)
    return jnp.einsum('btef,efd->btd', hid, lw['w_exp_down'])


def decoder_layer(x, c, past, lw):
    mod = jnp.einsum('bd,dn->bn', jax.nn.silu(c), lw['w_ada']) + lw['b_ada']
    shift1, scale1, gate1, shift2, scale2, gate2 = [m[:, None, :] for m in jnp.split(mod, 6, axis=-1)]
    h = layer_norm(x) * (1 + scale1) + shift1
    mix, rows = token_mixer(h, past, lw)
    x = layer_norm(DEEPNORM_ALPHA * x + gate1 * mix) * lw['ln1_g'] + lw['ln1_b']
    h = layer_norm(x) * (1 + scale2) + shift2
    x = layer_norm(DEEPNORM_ALPHA * x + gate2 * hier_moe(h, lw)) * lw['ln2_g'] + lw['ln2_b']
    return x, rows


def setup_inputs(seed: int = 0) -> dict:
    key = jax.random.key(seed)
    ks = iter(jax.random.split(key, 48))

    def nrm(shape, scale):
        return jax.random.normal(next(ks), shape, F32) * scale

    def gain(shape):
        return 1.0 + nrm(shape, 0.02)

    L = DEPTH
    n_pages = PAST_LEN // PAGE_SIZE
    n_used = DEC_BATCH * n_pages
    n_pool = (5 * n_used) // 4
    page_table = jax.random.permutation(next(ks), n_pool)[:n_used].reshape(DEC_BATCH, n_pages).astype(jnp.int32)
    return {
        'x_prompt': nrm((BATCH, SEQ, D_MODEL), 1.0),
        'x_sample': nrm((DEC_BATCH, DEC_SEQ, D_MODEL), 1.0),
        'cache_sb_k': nrm((L, n_pool, PAGE_SIZE, SB_KV_HEADS, SB_HEAD_DIM), 1.0),
        'cache_sb_v': nrm((L, n_pool, PAGE_SIZE, SB_KV_HEADS, SB_HEAD_DIM), 1.0),
        'cache_mla_latent': nrm((L, n_pool, PAGE_SIZE, MLA_KV_LORA), 1.0),
        'cache_mla_krope': nrm((L, n_pool, PAGE_SIZE, MLA_ROPE_DIM), 1.0),
        'state_conv': nrm((L, DEC_BATCH, CONV_WIDTH - 1, CONV_CH), 0.5),
        'state_pool': nrm((L, DEC_BATCH, POOL_STATE, POOL_CH), 1.0),
        'page_table': page_table,
        'c_prompt': nrm((BATCH, D_MODEL), 1.0),
        'c_sample': nrm((DEC_BATCH, D_MODEL), 1.0),
        'w_ada': nrm((L, D_MODEL, 6 * D_MODEL), D_MODEL ** -0.5),
        'b_ada': nrm((L, 6 * D_MODEL), 0.01),
        'w_in': nrm((L, D_MODEL, N_IN), D_MODEL ** -0.5),
        'conv_w': nrm((L, CONV_WIDTH, CONV_CH), CONV_WIDTH ** -0.5),
        'conv_b': nrm((L, CONV_CH), 0.01),
        'conv_ln_g': gain((L, CONV_CH)),
        'conv_ln_b': nrm((L, CONV_CH), 0.01),
        'mla_q_norm': gain((L, MLA_Q_LORA)),
        'w_uq': nrm((L, MLA_Q_LORA, MLA_HEADS, MLA_NOPE_DIM + MLA_ROPE_DIM), MLA_Q_LORA ** -0.5),
        'mla_kv_norm': gain((L, MLA_KV_LORA)),
        'w_uk': nrm((L, MLA_KV_LORA, MLA_HEADS, MLA_NOPE_DIM), MLA_KV_LORA ** -0.5),
        'w_uv': nrm((L, MLA_KV_LORA, MLA_HEADS, MLA_V_DIM), MLA_KV_LORA ** -0.5),
        'pool_w': nrm((L, len(POOL_WINDOWS), POOL_GROUP, POOL_GROUP), POOL_GROUP ** -0.5),
        'pool_scale': gain((L, POOL_CH)),
        'w_branch': nrm((L, N_BRANCH, BRANCH_W, D_MODEL), DEEPNORM_BETA * BRANCH_W ** -0.5),
        'w_out': nrm((L, D_MODEL, D_MODEL), DEEPNORM_BETA * D_MODEL ** -0.5),
        'ln1_g': gain((L, D_MODEL)),
        'ln1_b': nrm((L, D_MODEL), 0.01),
        'w_router_grp': nrm((L, D_MODEL, N_GROUPS), D_MODEL ** -0.5),
        'b_router_grp': nrm((L, N_GROUPS), 0.01),
        'w_router_exp': nrm((L, D_MODEL, N_GROUPS, EXPERTS_PER_GROUP), D_MODEL ** -0.5),
        'b_router_exp': nrm((L, N_GROUPS, EXPERTS_PER_GROUP), 0.01),
        'w_exp_gate': nrm((L, N_EXPERTS, D_MODEL, EXPERT_FF), D_MODEL ** -0.5),
        'w_exp_up': nrm((L, N_EXPERTS, D_MODEL, EXPERT_FF), D_MODEL ** -0.5),
        'w_exp_down': nrm((L, N_EXPERTS, EXPERT_FF, D_MODEL), DEEPNORM_BETA * EXPERT_FF ** -0.5),
        'ln2_g': gain((L, D_MODEL)),
        'ln2_b': nrm((L, D_MODEL), 0.01),
    }


def reference(x_prompt, x_sample, cache_sb_k, cache_sb_v, cache_mla_latent, cache_mla_krope, state_conv, state_pool,
              page_table, c_prompt, c_sample, w_ada, b_ada, w_in, conv_w, conv_b, conv_ln_g, conv_ln_b, mla_q_norm,
              w_uq, mla_kv_norm, w_uk, w_uv, pool_w, pool_scale, w_branch, w_out, ln1_g, ln1_b, w_router_grp,
              b_router_grp, w_router_exp, b_router_exp, w_exp_gate, w_exp_up, w_exp_down, ln2_g, ln2_b):
    bp = x_prompt.shape[0]
    db = x_sample.shape[0]
    dt = x_prompt.dtype
    past_len = page_table.shape[1] * cache_sb_k.shape[2]
    yp, ys = x_prompt, x_sample
    rows_prompt = [[] for _ in range(6)]
    rows_sample = [[] for _ in range(6)]
    for l in range(DEPTH):
        lw = {
            'w_ada': w_ada[l], 'b_ada': b_ada[l], 'w_in': w_in[l], 'conv_w': conv_w[l], 'conv_b': conv_b[l],
            'conv_ln_g': conv_ln_g[l], 'conv_ln_b': conv_ln_b[l], 'mla_q_norm': mla_q_norm[l], 'w_uq': w_uq[l],
            'mla_kv_norm': mla_kv_norm[l], 'w_uk': w_uk[l], 'w_uv': w_uv[l], 'pool_w': pool_w[l],
            'pool_scale': pool_scale[l], 'w_branch': w_branch[l], 'w_out': w_out[l], 'ln1_g': ln1_g[l],
            'ln1_b': ln1_b[l], 'w_router_grp': w_router_grp[l], 'b_router_grp': b_router_grp[l],
            'w_router_exp': w_router_exp[l], 'b_router_exp': b_router_exp[l], 'w_exp_gate': w_exp_gate[l],
            'w_exp_up': w_exp_up[l], 'w_exp_down': w_exp_down[l], 'ln2_g': ln2_g[l], 'ln2_b': ln2_b[l],
        }
        past_p = (jnp.zeros((bp, 0, SB_KV_HEADS, SB_HEAD_DIM), dt), jnp.zeros((bp, 0, SB_KV_HEADS, SB_HEAD_DIM), dt),
                  jnp.zeros((bp, 0, MLA_KV_LORA), dt), jnp.zeros((bp, 0, MLA_ROPE_DIM), dt),
                  jnp.zeros((bp, CONV_WIDTH - 1, CONV_CH), dt), jnp.zeros((bp, POOL_STATE, POOL_CH), dt))
        past_s = (cache_sb_k[l, page_table].reshape(db, past_len, SB_KV_HEADS, SB_HEAD_DIM),
                  cache_sb_v[l, page_table].reshape(db, past_len, SB_KV_HEADS, SB_HEAD_DIM),
                  cache_mla_latent[l, page_table].reshape(db, past_len, MLA_KV_LORA),
                  cache_mla_krope[l, page_table].reshape(db, past_len, MLA_ROPE_DIM),
                  state_conv[l], state_pool[l])
        yp, new_p = decoder_layer(yp, c_prompt, past_p, lw)
        ys, new_s = decoder_layer(ys, c_sample, past_s, lw)
        for i in range(6):
            rows_prompt[i].append(new_p[i])
            rows_sample[i].append(new_s[i])
    new_sb_k_prompt, new_sb_v_prompt, new_mla_latent_prompt, new_mla_krope_prompt, new_conv_prompt, new_pool_prompt = [
        jnp.stack(r) for r in rows_prompt]
    new_sb_k_sample, new_sb_v_sample, new_mla_latent_sample, new_mla_krope_sample, new_conv_sample, new_pool_sample = [
        jnp.stack(r) for r in rows_sample]
    return (yp, ys, new_sb_k_prompt, new_sb_v_prompt, new_mla_latent_prompt, new_mla_krope_prompt, new_conv_prompt,
            new_pool_prompt, new_sb_k_sample, new_sb_v_sample, new_mla_latent_sample, new_mla_krope_sample,
            new_conv_sample, new_pool_sample)
```

```python
import functools

import jax
import jax.numpy as jnp
from jax import lax
from jax.experimental import pallas as pl
from jax.experimental.pallas import tpu as pltpu

F32 = jnp.float32
BF16 = jnp.bfloat16

D_MODEL = 2048
BRANCH_W = 512
HEAD_DIM = 64
SB_HEADS = 8
SB_KV_HEADS = 2
SB_GROUP = 4
CONV_WIDTH = 31
MLA_HEADS = 8
MLA_NOPE = 64
MLA_ROPE = 32
MLA_Q_LORA = 512
MLA_KV_LORA = 256
MLA_SCALE = (MLA_NOPE + MLA_ROPE) ** -0.5
ROPE_THETA = 10000.0
POOL_WINDOWS = (2, 4, 8, 16)
POOL_GROUP = 128
POOL_STATE = 15
N_GROUPS = 4
EXPERTS_PER_GROUP = 8
N_EXPERTS = 32
EXPERT_FF = 256
LN_EPS = 1e-5
RMS_EPS = 1e-6
DEPTH_FOR_DEEPNORM = 4
DEEPNORM_ALPHA = (2 * DEPTH_FOR_DEEPNORM) ** 0.25
NEG_BIG = -1e30

LANE = 128
HALO = 32
VMEM_LIMIT = 56 * 1024 * 1024

ZC_Q = 0
ZC_K = 1024
ZC_V = 1280
ZC_GLU = 1536
ZC_CQ = 2560
ZC_CKV = 3072
ZC_KR = 3328
ZC_KRS = 3456
ZC_POOL = 3584
Z_SMALL = 4096

TM_ROWS = 512
TQ_SB = 256
TK_SB = 256
TQ_MLA = 256
TK_MLA = 256
TT_MIX = 512
DEC_PAGES_PER_STEP = 8


def _cparams(sem):
    return pltpu.CompilerParams(dimension_semantics=sem, vmem_limit_bytes=VMEM_LIMIT)


def _ln(x):
    mu = jnp.mean(x, axis=-1, keepdims=True)
    xc = x - mu
    var = jnp.mean(xc * xc, axis=-1, keepdims=True)
    return xc * lax.rsqrt(var + LN_EPS)


def _softplus(z):
    return jnp.maximum(z, 0.0) + jnp.log1p(jnp.exp(-jnp.abs(z)))


def _dot_nt(a, b):
    return lax.dot_general(a, b, (((1,), (1,)), ((), ())), preferred_element_type=F32)


def _ada_kernel(c_ref, w_ref, b_ref, o_ref):
    c = c_ref[...]
    s = (c * jax.nn.sigmoid(c)).astype(BF16)
    o_ref[0] = jnp.dot(s, w_ref[0].astype(BF16), preferred_element_type=F32) + b_ref[0]


def _ada(c_all, w_ada, b_ada):
    L, D, N = w_ada.shape
    M = c_all.shape[0]
    tn = 1024
    return pl.pallas_call(
        _ada_kernel,
        out_shape=jax.ShapeDtypeStruct((L, M, N), F32),
        grid=(L, N // tn),
        in_specs=[
            pl.BlockSpec((M, D), lambda l, n: (0, 0)),
            pl.BlockSpec((1, D, tn), lambda l, n: (l, 0, n)),
            pl.BlockSpec((1, 1, tn), lambda l, n: (l, 0, n)),
        ],
        out_specs=pl.BlockSpec((1, M, tn), lambda l, n: (l, 0, n)),
        compiler_params=_cparams(("parallel", "parallel")),
        name="ada",
    )(c_all, w_ada, b_ada.reshape(L, 1, N))


def _mod_spec(mod, chunk, tr, nd):
    gm, rm, _ = mod.shape
    if rm == 1:
        if nd == 3:
            return pl.BlockSpec((1, 1, D_MODEL), lambda g, r, n: (g, 0, chunk))
        return pl.BlockSpec((1, 1, D_MODEL), lambda g, r: (g, 0, chunk))
    if nd == 3:
        return pl.BlockSpec((1, tr, D_MODEL), lambda g, r, n: (0, r, chunk))
    return pl.BlockSpec((1, tr, D_MODEL), lambda g, r: (0, r, chunk))


def _row_tiling(x3, mod):
    G, R, _ = x3.shape
    if mod.shape[1] == 1:
        return 1, min(TM_ROWS, R)
    return G, R


def _lnmm_kernel(x_ref, sc_ref, sh_ref, w_ref, o_ref, h_scr, *, act):
    @pl.when(pl.program_id(2) == 0)
    def _():
        h = _ln(x_ref[...]) * (1.0 + sc_ref[...]) + sh_ref[...]
        h_scr[...] = h.reshape(h_scr.shape).astype(BF16)

    y = jnp.dot(h_scr[...], w_ref[...], preferred_element_type=F32)
    if act == "sigmoid":
        y = jax.nn.sigmoid(y)
    o_ref[...] = y.reshape(o_ref.shape).astype(o_ref.dtype)


def _lnmm(x3, mod, sc_chunk, sh_chunk, w, act, out_dtype, tn):
    G, R, D = x3.shape
    N = w.shape[1]
    gb, tr = _row_tiling(x3, mod)
    return pl.pallas_call(
        functools.partial(_lnmm_kernel, act=act),
        out_shape=jax.ShapeDtypeStruct((G, R, N), out_dtype),
        grid=(G // gb, R // tr, N // tn),
        in_specs=[
            pl.BlockSpec((gb, tr, D), lambda g, r, n: (g, r, 0)),
            _mod_spec(mod, sc_chunk, tr, 3),
            _mod_spec(mod, sh_chunk, tr, 3),
            pl.BlockSpec((D, tn), lambda g, r, n: (0, n)),
        ],
        out_specs=pl.BlockSpec((gb, tr, tn), lambda g, r, n: (g, r, n)),
        scratch_shapes=[pltpu.VMEM((gb * tr, D), BF16)],
        compiler_params=_cparams(("parallel", "parallel", "arbitrary")),
        name="lnmm_" + str(act),
    )(x3, mod, mod, w)


def _sb_tri(tk):
    j = lax.broadcasted_iota(jnp.int32, (tk, tk), 0)
    s = lax.broadcasted_iota(jnp.int32, (tk, tk), 1)
    u = (j > s).astype(BF16)
    return jnp.concatenate([u, jnp.ones((tk, LANE), BF16)], axis=1)


def _sb_block(qs, kb, vb, u2, mask, carry, tk):
    m = qs.shape[0]
    z = _dot_nt(qs, kb)
    sp = _softplus(z)
    lk = -sp if mask is None else jnp.where(mask, -sp, 0.0)
    hi = lk.astype(BF16)
    lo = (lk - hi.astype(F32)).astype(BF16)
    r = jnp.dot(jnp.concatenate([hi, lo], axis=0), u2, preferred_element_type=F32)
    r = r[:m] + r[m:]
    tail = r[:, :tk] + jnp.concatenate([carry] * (tk // LANE), axis=1)
    a = jnp.exp(z - sp + tail)
    if mask is not None:
        a = jnp.where(mask, a, 0.0)
    pv = jnp.dot(a.astype(BF16), vb, preferred_element_type=F32)
    return pv, carry + r[:, tk:]


def _sbp_kernel(q_ref, k_ref, v_ref, u2_ref, o_ref, acc_scr, car_scr, *, tq, tk):
    i = pl.program_id(2)
    q = q_ref[0]
    qs = jnp.concatenate([q[:, g * LANE:(g + 1) * LANE] for g in range(SB_GROUP)], axis=0)
    qs = (qs * (HEAD_DIM ** -0.5)).astype(BF16)
    acc_scr[...] = jnp.zeros_like(acc_scr)
    car_scr[...] = jnp.zeros_like(car_scr)
    nkb = (i * tq + tq - 1) // tk + 1
    row = lax.broadcasted_iota(jnp.int32, (tq, tk), 0)
    qpos = i * tq + jnp.concatenate([row] * SB_GROUP, axis=0)
    col = lax.broadcasted_iota(jnp.int32, (SB_GROUP * tq, tk), 1)
    u2 = u2_ref[...]

    def body(jj, c):
        ks = pl.multiple_of((nkb - 1 - jj) * tk, tk)
        kb = k_ref[0, pl.ds(ks, tk), :].astype(BF16)
        vb = v_ref[0, pl.ds(ks, tk), :].astype(BF16)
        mask = (ks + col) < qpos
        pv, car = _sb_block(qs, kb, vb, u2, mask, car_scr[...], tk)
        acc_scr[...] += pv
        car_scr[...] = car
        return c

    lax.fori_loop(0, nkb, body, 0)
    acc = acc_scr[...]
    for g in range(SB_GROUP):
        o_ref[0, :, g * LANE:(g + 1) * LANE] = acc[g * tq:(g + 1) * tq].astype(o_ref.dtype)


def _sb_prompt(z3):
    B, T, _ = z3.shape
    tq = min(TQ_SB, T)
    tk = min(TK_SB, T)
    qw = SB_GROUP * LANE
    return pl.pallas_call(
        functools.partial(_sbp_kernel, tq=tq, tk=tk),
        out_shape=jax.ShapeDtypeStruct((B, T, SB_HEADS * LANE), BF16),
        grid=(B, SB_KV_HEADS, T // tq),
        in_specs=[
            pl.BlockSpec((1, tq, qw), lambda b, h, i: (b, i, ZC_Q // qw + h)),
            pl.BlockSpec((1, T, LANE), lambda b, h, i: (b, 0, ZC_K // LANE + h)),
            pl.BlockSpec((1, T, LANE), lambda b, h, i: (b, 0, ZC_V // LANE + h)),
            pl.BlockSpec((tk, tk + LANE), lambda b, h, i: (0, 0)),
        ],
        out_specs=pl.BlockSpec((1, tq, qw), lambda b, h, i: (b, i, h)),
        scratch_shapes=[pltpu.VMEM((SB_GROUP * tq, LANE), F32), pltpu.VMEM((SB_GROUP * tq, LANE), F32)],
        compiler_params=_cparams(("parallel", "parallel", "parallel")),
        name="sb_prompt",
    )(z3, z3, z3, _sb_tri(tk))


def _mlaprep_kernel(cq_ref, ckv_ref, kr_ref, krs_ref, ct_ref, st_ref, gq_ref, gkv_ref, wq_ref, wuk_ref,
                    qcat_ref, lat_ref, kro_ref):
    cq = cq_ref[...]
    cqn = (cq * lax.rsqrt(jnp.mean(cq * cq, axis=-1, keepdims=True) + RMS_EPS)) * gq_ref[...]
    q3 = jnp.dot(cqn.astype(BF16), wq_ref[...], preferred_element_type=F32)
    ct = ct_ref[...]
    st = st_ref[...]
    hw = MLA_HEADS * LANE
    qw = MLA_KV_LORA + LANE
    for h in range(MLA_HEADS):
        qn = q3[:, h * LANE:(h + 1) * LANE].astype(BF16)
        ql = jnp.dot(qn, wuk_ref[h], preferred_element_type=F32)
        qr = q3[:, hw + h * LANE:hw + (h + 1) * LANE] * ct + q3[:, 2 * hw + h * LANE:2 * hw + (h + 1) * LANE] * st
        qcat_ref[:, h * qw:h * qw + MLA_KV_LORA] = ql.astype(BF16)
        qcat_ref[:, h * qw + MLA_KV_LORA:(h + 1) * qw] = qr.astype(BF16)
    ckv = ckv_ref[...]
    lat_ref[...] = (ckv * lax.rsqrt(jnp.mean(ckv * ckv, axis=-1, keepdims=True) + RMS_EPS)) * gkv_ref[...]
    kro_ref[...] = kr_ref[...] * ct + krs_ref[...] * st


def _mla_prep(z2, cos_t, sin_t, gq, gkv, wq3, wukT):
    M = z2.shape[0]
    tm = min(TM_ROWS, M)
    qw = MLA_KV_LORA + LANE
    return pl.pallas_call(
        _mlaprep_kernel,
        out_shape=(jax.ShapeDtypeStruct((M, MLA_HEADS * qw), BF16),
                   jax.ShapeDtypeStruct((M, MLA_KV_LORA), F32),
                   jax.ShapeDtypeStruct((M, LANE), F32)),
        grid=(M // tm,),
        in_specs=[
            pl.BlockSpec((tm, MLA_Q_LORA), lambda i: (i, ZC_CQ // MLA_Q_LORA)),
            pl.BlockSpec((tm, MLA_KV_LORA), lambda i: (i, ZC_CKV // MLA_KV_LORA)),
            pl.BlockSpec((tm, LANE), lambda i: (i, ZC_KR // LANE)),
            pl.BlockSpec((tm, LANE), lambda i: (i, ZC_KRS // LANE)),
            pl.BlockSpec((tm, LANE), lambda i: (i, 0)),
            pl.BlockSpec((tm, LANE), lambda i: (i, 0)),
            pl.BlockSpec((1, MLA_Q_LORA), lambda i: (0, 0)),
            pl.BlockSpec((1, MLA_KV_LORA), lambda i: (0, 0)),
            pl.BlockSpec((MLA_Q_LORA, 3 * MLA_HEADS * LANE), lambda i: (0, 0)),
            pl.BlockSpec((MLA_HEADS, LANE, MLA_KV_LORA), lambda i: (0, 0, 0)),
        ],
        out_specs=(pl.BlockSpec((tm, MLA_HEADS * qw), lambda i: (i, 0)),
                   pl.BlockSpec((tm, MLA_KV_LORA), lambda i: (i, 0)),
                   pl.BlockSpec((tm, LANE), lambda i: (i, 0))),
        compiler_params=_cparams(("parallel",)),
        name="mla_prep",
    )(z2, z2, z2, z2, cos_t, sin_t, gq, gkv, wq3, wukT)


def _mla_update(qh, kcat_t, mask, m_prev, l_prev, acc_prev, latb):
    s = (_dot_nt(qh[:, :MLA_KV_LORA], kcat_t[0]) +
         _dot_nt(qh[:, MLA_KV_LORA:MLA_KV_LORA + kcat_t[1].shape[1]], kcat_t[1])) * MLA_SCALE
    if mask is not None:
        s = jnp.where(mask, s, NEG_BIG)
    m_new = jnp.maximum(m_prev, jnp.max(s, axis=-1, keepdims=True))
    p = jnp.exp(s - m_new)
    alpha = jnp.exp(m_prev - m_new)
    l_new = alpha * l_prev + jnp.sum(p, axis=-1, keepdims=True)
    acc_new = alpha * acc_prev + jnp.dot(p.astype(BF16), latb, preferred_element_type=F32)
    return m_new, l_new, acc_new


def _mlap_kernel(q_ref, lat_ref, kr_ref, wuv_ref, o_ref, m_scr, l_scr, acc_scr, *, tq, tk):
    i = pl.program_id(1)
    m_scr[...] = jnp.full_like(m_scr, NEG_BIG)
    l_scr[...] = jnp.zeros_like(l_scr)
    acc_scr[...] = jnp.zeros_like(acc_scr)
    nkb = (i * tq + tq - 1) // tk + 1
    qpos = i * tq + lax.broadcasted_iota(jnp.int32, (tq, tk), 0)
    col = lax.broadcasted_iota(jnp.int32, (tq, tk), 1)
    qw = MLA_KV_LORA + LANE

    def body(j, c):
        ks = pl.multiple_of(j * tk, tk)
        latb = lat_ref[0, pl.ds(ks, tk), :].astype(BF16)
        krb = kr_ref[0, pl.ds(ks, tk), :].astype(BF16)
        mask = (ks + col) <= qpos
        for h in range(MLA_HEADS):
            qh = q_ref[0, :, h * qw:(h + 1) * qw]
            m_new, l_new, acc_new = _mla_update(qh, (latb, krb), mask, m_scr[h], l_scr[h], acc_scr[h], latb)
            m_scr[h] = m_new
            l_scr[h] = l_new
            acc_scr[h] = acc_new
        return c

    lax.fori_loop(0, nkb, body, 0)
    for h in range(MLA_HEADS):
        ol = (acc_scr[h] / l_scr[h]).astype(BF16)
        o_ref[0, :, h * LANE:(h + 1) * LANE] = jnp.dot(ol, wuv_ref[h], preferred_element_type=F32).astype(o_ref.dtype)


def _mla_prompt(qcat3, lat3, kro3, wuv):
    B, T, QW = qcat3.shape
    tq = min(TQ_MLA, T)
    tk = min(TK_MLA, T)
    return pl.pallas_call(
        functools.partial(_mlap_kernel, tq=tq, tk=tk),
        out_shape=jax.ShapeDtypeStruct((B, T, MLA_HEADS * LANE), BF16),
        grid=(B, T // tq),
        in_specs=[
            pl.BlockSpec((1, tq, QW), lambda b, i: (b, i, 0)),
            pl.BlockSpec((1, T, MLA_KV_LORA), lambda b, i: (b, 0, 0)),
            pl.BlockSpec((1, T, LANE), lambda b, i: (b, 0, 0)),
            pl.BlockSpec((MLA_HEADS, MLA_KV_LORA, LANE), lambda b, i: (0, 0, 0)),
        ],
        out_specs=pl.BlockSpec((1, tq, MLA_HEADS * LANE), lambda b, i: (b, i, 0)),
        scratch_shapes=[pltpu.VMEM((MLA_HEADS, tq, 1), F32), pltpu.VMEM((MLA_HEADS, tq, 1), F32),
                        pltpu.VMEM((MLA_HEADS, tq, MLA_KV_LORA), F32)],
        compiler_params=_cparams(("parallel", "parallel")),
        name="mla_prompt",
    )(qcat3, lat3, kro3, wuv)


def _conv_pool_core(ext_c, ext_p, rows, bn, hc, hp, cw_ref, cb_ref, lg_ref, lb_ref, pw_ref, ps_ref, pos,
                    ob_ref, od_ref):
    y = jnp.zeros((rows, BRANCH_W), F32) + cb_ref[...]
    for k in range(CONV_WIDTH):
        y = y + cw_ref[k:k + 1, :] * ext_c[pl.ds(hc - (CONV_WIDTH - 1 - k) * bn, rows), :]
    y = _ln(y) * lg_ref[...] + lb_ref[...]
    ob_ref[...] = (y * jax.nn.sigmoid(y)).reshape(ob_ref.shape).astype(ob_ref.dtype)
    outs = []
    for g, w in enumerate(POOL_WINDOWS):
        sl = slice(g * POOL_GROUP, (g + 1) * POOL_GROUP)
        x0 = ext_p[pl.ds(hp, rows), sl]
        ws = x0
        for d in range(1, w):
            ws = ws + ext_p[pl.ds(hp - d * bn, rows), sl]
        cnt = jnp.minimum(pos + 1, w).astype(F32)
        diff = ws / cnt - x0
        mixed = jnp.dot(diff.astype(BF16), pw_ref[g], preferred_element_type=F32) * ps_ref[:, sl]
        outs.append(mixed)
    od_ref[...] = jnp.concatenate(outs, axis=1).reshape(od_ref.shape).astype(od_ref.dtype)


def _mixp_kernel(ga_ref, gb_ref, gah_ref, gbh_ref, pu_ref, puh_ref, cw_ref, cb_ref, lg_ref, lb_ref, pw_ref, ps_ref,
                 ob_ref, od_ref, tail_ref, ext_c, ext_p, *, tt):
    i = pl.program_id(1)
    keep = (i > 0).astype(F32)
    ext_c[0:HALO, :] = gah_ref[0] * jax.nn.sigmoid(gbh_ref[0]) * keep
    ext_c[HALO:HALO + tt, :] = ga_ref[0] * jax.nn.sigmoid(gb_ref[0])
    ext_p[0:HALO, :] = puh_ref[0] * keep
    ext_p[HALO:HALO + tt, :] = pu_ref[0]
    pos = i * tt + lax.broadcasted_iota(jnp.int32, (tt, 1), 0)
    _conv_pool_core(ext_c, ext_p, tt, 1, HALO, HALO, cw_ref, cb_ref, lg_ref, lb_ref, pw_ref, ps_ref, pos,
                    ob_ref, od_ref)
    tail_ref[0] = ext_c[tt:tt + HALO, :]


def _mix_prompt(z3, cw, cb, lg, lb, pw, ps):
    B, T, _ = z3.shape
    tt = min(TT_MIX, T)
    hb = tt // HALO
    w = BRANCH_W

    def halo_map(col):
        return lambda b, i: (b, jnp.maximum(i * hb - 1, 0), col)

    const2 = lambda b, i: (0, 0)
    return pl.pallas_call(
        functools.partial(_mixp_kernel, tt=tt),
        out_shape=(jax.ShapeDtypeStruct((B, T, w), BF16), jax.ShapeDtypeStruct((B, T, w), BF16),
                   jax.ShapeDtypeStruct((B, HALO, w), F32)),
        grid=(B, T // tt),
        in_specs=[
            pl.BlockSpec((1, tt, w), lambda b, i: (b, i, ZC_GLU // w)),
            pl.BlockSpec((1, tt, w), lambda b, i: (b, i, ZC_GLU // w + 1)),
            pl.BlockSpec((1, HALO, w), halo_map(ZC_GLU // w)),
            pl.BlockSpec((1, HALO, w), halo_map(ZC_GLU // w + 1)),
            pl.BlockSpec((1, tt, w), lambda b, i: (b, i, ZC_POOL // w)),
            pl.BlockSpec((1, HALO, w), halo_map(ZC_POOL // w)),
            pl.BlockSpec((CONV_WIDTH, w), const2),
            pl.BlockSpec((1, w), const2),
            pl.BlockSpec((1, w), const2),
            pl.BlockSpec((1, w), const2),
            pl.BlockSpec((len(POOL_WINDOWS), POOL_GROUP, POOL_GROUP), lambda b, i: (0, 0, 0)),
            pl.BlockSpec((1, w), const2),
        ],
        out_specs=(pl.BlockSpec((1, tt, w), lambda b, i: (b, i, 0)),
                   pl.BlockSpec((1, tt, w), lambda b, i: (b, i, 0)),
                   pl.BlockSpec((1, HALO, w), lambda b, i: (b, 0, 0))),
        scratch_shapes=[pltpu.VMEM((HALO + tt, w), F32), pltpu.VMEM((HALO + tt, w), F32)],
        compiler_params=_cparams(("parallel", "arbitrary")),
        name="mix_prompt",
    )(z3, z3, z3, z3, z3, z3, cw, cb, lg, lb, pw, ps)


def _mixs_kernel(ga_ref, gb_ref, sc_ref, pu_ref, sp_ref, cw_ref, cb_ref, lg_ref, lb_ref, pw_ref, ps_ref,
                 ob_ref, od_ref, u_ref, ext_c, ext_p, *, rows, bn, pos0):
    hc = sc_ref.shape[0]
    hp = sp_ref.shape[0]
    u = ga_ref[...] * jax.nn.sigmoid(gb_ref[...])
    u_ref[...] = u
    ext_c[0:hc, :] = sc_ref[...]
    ext_c[hc:hc + rows, :] = u
    ext_p[0:hp, :] = sp_ref[...]
    ext_p[hp:hp + rows, :] = pu_ref[...]
    pos = pos0 + lax.broadcasted_iota(jnp.int32, (rows, 1), 0) // bn
    _conv_pool_core(ext_c, ext_p, rows, bn, hc, hp, cw_ref, cb_ref, lg_ref, lb_ref, pw_ref, ps_ref, pos,
                    ob_ref, od_ref)


def _mix_sample(z2, conv_state_tm, pool_state_tm, bn, pos0, cw, cb, lg, lb, pw, ps):
    rows = z2.shape[0]
    w = BRANCH_W
    hc = conv_state_tm.shape[0]
    hp = pool_state_tm.shape[0]
    const2 = lambda i: (0, 0)
    return pl.pallas_call(
        functools.partial(_mixs_kernel, rows=rows, bn=bn, pos0=pos0),
        out_shape=(jax.ShapeDtypeStruct((rows, w), BF16), jax.ShapeDtypeStruct((rows, w), BF16),
                   jax.ShapeDtypeStruct((rows, w), F32)),
        grid=(1,),
        in_specs=[
            pl.BlockSpec((rows, w), lambda i: (0, ZC_GLU // w)),
            pl.BlockSpec((rows, w), lambda i: (0, ZC_GLU // w + 1)),
            pl.BlockSpec((hc, w), const2),
            pl.BlockSpec((rows, w), lambda i: (0, ZC_POOL // w)),
            pl.BlockSpec((hp, w), const2),
            pl.BlockSpec((CONV_WIDTH, w), const2),
            pl.BlockSpec((1, w), const2),
            pl.BlockSpec((1, w), const2),
            pl.BlockSpec((1, w), const2),
            pl.BlockSpec((len(POOL_WINDOWS), POOL_GROUP, POOL_GROUP), lambda i: (0, 0, 0)),
            pl.BlockSpec((1, w), const2),
        ],
        out_specs=(pl.BlockSpec((rows, w), const2), pl.BlockSpec((rows, w), const2), pl.BlockSpec((rows, w), const2)),
        scratch_shapes=[pltpu.VMEM((hc + rows, w), F32), pltpu.VMEM((hp + rows, w), F32)],
        compiler_params=_cparams(("arbitrary",)),
        name="mix_sample",
    )(z2, z2, conv_state_tm, z2, pool_state_tm, cw, cb, lg, lb, pw, ps)


def _dec_kernel(pt_ref, qsb_ref, qm_ref, kn_ref, vn_ref, ln_ref, rn_ref, u2_ref, wuv_ref, *rest, cpages, dec_seq):
    k_refs = rest[0:cpages]
    v_refs = rest[cpages:2 * cpages]
    l_refs = rest[2 * cpages:3 * cpages]
    r_refs = rest[3 * cpages:4 * cpages]
    oa_ref, oc_ref, sacc, scar, m_scr, l_scr, macc = rest[4 * cpages:]
    c = pl.program_id(1)
    nq = qsb_ref.shape[1]
    qs = (qsb_ref[0] * (HEAD_DIM ** -0.5)).astype(BF16)
    qm = qm_ref[0]
    u2 = u2_ref[...]

    def sb_step(kp, vp, mask):
        pv, car = _sb_block(qs, kp.astype(BF16), vp.astype(BF16), u2, mask, scar[...], LANE)
        sacc[...] += pv
        scar[...] = car

    def mla_step(lp, rp, mask):
        latb = lp.astype(BF16)
        m_new, l_new, acc_new = _mla_update(qm, (latb, rp.astype(BF16)), mask, m_scr[...], l_scr[...], macc[...], latb)
        m_scr[...] = m_new
        l_scr[...] = l_new
        macc[...] = acc_new

    @pl.when(c == 0)
    def _():
        sacc[...] = jnp.zeros_like(sacc)
        scar[...] = jnp.zeros_like(scar)
        m_scr[...] = jnp.full_like(m_scr, NEG_BIG)
        l_scr[...] = jnp.zeros_like(l_scr)
        macc[...] = jnp.zeros_like(macc)
        t = lax.broadcasted_iota(jnp.int32, (nq, LANE), 0) % dec_seq
        s = lax.broadcasted_iota(jnp.int32, (nq, LANE), 1)
        sb_step(kn_ref[0], vn_ref[0], s < t)
        mla_step(ln_ref[0], rn_ref[0], s <= t)

    for p in reversed(range(cpages)):
        sb_step(k_refs[p][0, 0], v_refs[p][0, 0], None)
        mla_step(l_refs[p][0, 0], r_refs[p][0, 0], None)

    @pl.when(c == pl.num_programs(1) - 1)
    def _():
        row = lax.broadcasted_iota(jnp.int32, (nq, LANE), 0)
        lane = lax.broadcasted_iota(jnp.int32, (nq, LANE), 1)
        kv_of_row = row // (dec_seq * SB_GROUP)
        oa_ref[0] = jnp.where((lane // HEAD_DIM) == kv_of_row, sacc[...], 0.0).astype(oa_ref.dtype)
        ol = (macc[...] / l_scr[...]).astype(BF16)
        oc = jnp.zeros((nq, LANE), F32)
        head_of_row = row // dec_seq
        for h in range(MLA_HEADS):
            oh = jnp.dot(ol, wuv_ref[h], preferred_element_type=F32)
            oc = oc + jnp.where(head_of_row == h, oh, 0.0)
        oc_ref[0] = oc.astype(oc_ref.dtype)


def _dec_attention(layer, page_table, q_sb, q_mla, k_new, v_new, lat_new, kr_new, cache_k, cache_v, cache_lat,
                   cache_kr, wuv, dec_seq):
    DB, NP = page_table.shape
    cp = min(DEC_PAGES_PER_STEP, NP)
    nc = NP // cp
    nq = q_sb.shape[1]
    page = cache_k.shape[2]
    qw = q_mla.shape[2]

    def pmap(p):
        return lambda b, c, pt: (layer, pt[b * NP + (nc - 1 - c) * cp + p], 0, 0)

    bmap = lambda b, c, pt: (b, 0, 0)
    in_specs = [
        pl.BlockSpec((1, nq, LANE), bmap),
        pl.BlockSpec((1, nq, qw), bmap),
        pl.BlockSpec((1, LANE, LANE), bmap),
        pl.BlockSpec((1, LANE, LANE), bmap),
        pl.BlockSpec((1, LANE, MLA_KV_LORA), bmap),
        pl.BlockSpec((1, LANE, MLA_ROPE), bmap),
        pl.BlockSpec((LANE, 2 * LANE), lambda b, c, pt: (0, 0)),
        pl.BlockSpec((MLA_HEADS, MLA_KV_LORA, LANE), lambda b, c, pt: (0, 0, 0)),
    ]
    in_specs += [pl.BlockSpec((1, 1, page, LANE), pmap(p)) for p in range(cp)]
    in_specs += [pl.BlockSpec((1, 1, page, LANE), pmap(p)) for p in range(cp)]
    in_specs += [pl.BlockSpec((1, 1, page, MLA_KV_LORA), pmap(p)) for p in range(cp)]
    in_specs += [pl.BlockSpec((1, 1, page, MLA_ROPE), pmap(p)) for p in range(cp)]
    grid_spec = pltpu.PrefetchScalarGridSpec(
        num_scalar_prefetch=1,
        grid=(DB, nc),
        in_specs=in_specs,
        out_specs=(pl.BlockSpec((1, nq, LANE), bmap), pl.BlockSpec((1, nq, LANE), bmap)),
        scratch_shapes=[pltpu.VMEM((nq, LANE), F32), pltpu.VMEM((nq, LANE), F32), pltpu.VMEM((nq, 1), F32),
                        pltpu.VMEM((nq, 1), F32), pltpu.VMEM((nq, MLA_KV_LORA), F32)],
    )
    args = [page_table.reshape(-1), q_sb, q_mla, k_new, v_new, lat_new, kr_new, _sb_tri(LANE), wuv]
    args += [cache_k] * cp + [cache_v] * cp + [cache_lat] * cp + [cache_kr] * cp
    return pl.pallas_call(
        functools.partial(_dec_kernel, cpages=cp, dec_seq=dec_seq),
        out_shape=(jax.ShapeDtypeStruct((DB, nq, LANE), BF16), jax.ShapeDtypeStruct((DB, nq, LANE), BF16)),
        grid_spec=grid_spec,
        compiler_params=_cparams(("parallel", "arbitrary")),
        name="dec_attention",
    )(*args)


def _merge_kernel(oa_ref, ob_ref, oc_ref, od_ref, g_ref, wa_ref, wb_ref, wc_ref, wd_ref, o_ref):
    acc = None
    for i, (o, w) in enumerate(((oa_ref, wa_ref), (ob_ref, wb_ref), (oc_ref, wc_ref), (od_ref, wd_ref))):
        p = jnp.dot(o[...], w[...], preferred_element_type=F32)
        g = g_ref[:, i * D_MODEL:(i + 1) * D_MODEL].astype(F32)
        acc = g * p if acc is None else acc + g * p
    o_ref[...] = acc.astype(o_ref.dtype)


def _merge(oa, ob, oc, od, gates, wa, wb, wc, wd):
    M = oa.shape[0]
    tm = min(256, M)
    row = lambda i: (i, 0)
    const = lambda i: (0, 0)
    return pl.pallas_call(
        _merge_kernel,
        out_shape=jax.ShapeDtypeStruct((M, D_MODEL), BF16),
        grid=(M // tm,),
        in_specs=[pl.BlockSpec((tm, oa.shape[1]), row), pl.BlockSpec((tm, ob.shape[1]), row),
                  pl.BlockSpec((tm, oc.shape[1]), row), pl.BlockSpec((tm, od.shape[1]), row),
                  pl.BlockSpec((tm, gates.shape[1]), row),
                  pl.BlockSpec(wa.shape, const), pl.BlockSpec(wb.shape, const),
                  pl.BlockSpec(wc.shape, const), pl.BlockSpec(wd.shape, const)],
        out_specs=pl.BlockSpec((tm, D_MODEL), row),
        compiler_params=_cparams(("parallel",)),
        name="merge",
    )(oa, ob, oc, od, gates, wa, wb, wc, wd)


def _out_kernel(mg_ref, x_ref, g1_ref, sc2_ref, sh2_ref, w_ref, lg_ref, lb_ref, x1_ref, h2_ref):
    mg = mg_ref[...]
    mix = jnp.dot(mg.reshape(-1, D_MODEL), w_ref[...], preferred_element_type=F32).reshape(x_ref.shape)
    x1 = _ln(DEEPNORM_ALPHA * x_ref[...] + g1_ref[...] * mix) * lg_ref[...] + lb_ref[...]
    x1_ref[...] = x1
    h2_ref[...] = (_ln(x1) * (1.0 + sc2_ref[...]) + sh2_ref[...]).astype(h2_ref.dtype)


def _out_proj(merged3, x3, mod, w_out, ln_g, ln_b):
    G, R, D = x3.shape
    gb, tr = _row_tiling(x3, mod)
    if mod.shape[1] == 1:
        tr = min(256, tr)
    blk = pl.BlockSpec((gb, tr, D), lambda g, r: (g, r, 0))
    vec = pl.BlockSpec((1, 1, D), lambda g, r: (0, 0, 0))
    return pl.pallas_call(
        _out_kernel,
        out_shape=(jax.ShapeDtypeStruct((G, R, D), F32), jax.ShapeDtypeStruct((G, R, D), BF16)),
        grid=(G // gb, R // tr),
        in_specs=[blk, blk, _mod_spec(mod, 2, tr, 2), _mod_spec(mod, 4, tr, 2), _mod_spec(mod, 3, tr, 2),
                  pl.BlockSpec((D, D), lambda g, r: (0, 0)), vec, vec],
        out_specs=(blk, blk),
        compiler_params=_cparams(("parallel", "parallel")),
        name="out_proj",
    )(merged3, x3, mod, mod, mod, w_out, ln_g.reshape(1, 1, D), ln_b.reshape(1, 1, D))


def _route(h, wrg_ref, brg_ref, wre_ref, bre_ref):
    gl = jnp.dot(h, wrg_ref[...], preferred_element_type=F32) + brg_ref[...]
    el = jnp.dot(h, wre_ref[...], preferred_element_type=F32) + bre_ref[...]
    lane_i = lax.broadcasted_iota(jnp.int32, gl.shape, 1)
    lane = lane_i.astype(F32)
    big = float(1 << 20)
    gl = jnp.where(lane_i < N_GROUPS, gl, NEG_BIG)
    gmax = jnp.max(gl, axis=-1, keepdims=True)
    gidx = jnp.min(jnp.where(gl == gmax, lane, big), axis=-1, keepdims=True)
    gw = 1.0 / jnp.sum(jnp.exp(gl - gmax), axis=-1, keepdims=True)
    valid = ((lane_i // EXPERTS_PER_GROUP).astype(F32) == gidx) & (lane_i < N_EXPERTS)
    e1 = jnp.where(valid, el, NEG_BIG)
    l1 = jnp.max(e1, axis=-1, keepdims=True)
    i1 = jnp.min(jnp.where(e1 == l1, lane, big), axis=-1, keepdims=True)
    e2 = jnp.where(lane == i1, NEG_BIG, e1)
    l2 = jnp.max(e2, axis=-1, keepdims=True)
    i2 = jnp.min(jnp.where(e2 == l2, lane, big), axis=-1, keepdims=True)
    t = jnp.exp(l2 - l1)
    w1 = 1.0 / (1.0 + t)
    w2 = t / (1.0 + t)
    return gw * (jnp.where(lane == i1, w1, 0.0) + jnp.where(lane == i2, w2, 0.0))


def _moe_kernel(h_ref, x_ref, g2_ref, wrg_ref, brg_ref, wre_ref, bre_ref, wgu_ref, wd_ref, lg_ref, lb_ref,
                o_ref, comb_scr, acc_scr):
    e = pl.program_id(2)
    h = h_ref[...].reshape(-1, D_MODEL)

    @pl.when(e == 0)
    def _():
        comb_scr[...] = _route(h, wrg_ref, brg_ref, wre_ref, bre_ref)
        acc_scr[...] = jnp.zeros_like(acc_scr)

    comb = comb_scr[...]
    lane = lax.broadcasted_iota(jnp.int32, comb.shape, 1)
    ce = jnp.sum(jnp.where(lane == e, comb, 0.0), axis=-1, keepdims=True)
    au = jnp.dot(h, wgu_ref[0], preferred_element_type=F32)
    a = au[:, :EXPERT_FF]
    u = au[:, EXPERT_FF:]
    hid = (a * jax.nn.sigmoid(a)) * u * ce
    acc_scr[...] += jnp.dot(hid.astype(BF16), wd_ref[0], preferred_element_type=F32)

    @pl.when(e == pl.num_programs(2) - 1)
    def _():
        moe = acc_scr[...].reshape(x_ref.shape)
        o_ref[...] = _ln(DEEPNORM_ALPHA * x_ref[...] + g2_ref[...] * moe) * lg_ref[...] + lb_ref[...]


def _moe(h3, x3, mod, wrg, brg, wre, bre, wgu, wd, ln_g, ln_b):
    G, R, D = x3.shape
    gb, tr = _row_tiling(x3, mod)
    E = wgu.shape[0]
    blk = pl.BlockSpec((gb, tr, D), lambda g, r, e: (g, r, 0))
    vec = pl.BlockSpec((1, 1, D), lambda g, r, e: (0, 0, 0))
    c2 = lambda g, r, e: (0, 0)
    gm, rm, _ = mod.shape
    if rm == 1:
        g2_spec = pl.BlockSpec((1, 1, D), lambda g, r, e: (g, 0, 5))
    else:
        g2_spec = pl.BlockSpec((1, tr, D), lambda g, r, e: (0, r, 5))
    return pl.pallas_call(
        _moe_kernel,
        out_shape=jax.ShapeDtypeStruct((G, R, D), F32),
        grid=(G // gb, R // tr, E),
        in_specs=[blk, blk, g2_spec,
                  pl.BlockSpec((D, LANE), c2), pl.BlockSpec((1, LANE), c2),
                  pl.BlockSpec((D, LANE), c2), pl.BlockSpec((1, LANE), c2),
                  pl.BlockSpec((1, D, 2 * EXPERT_FF), lambda g, r, e: (e, 0, 0)),
                  pl.BlockSpec((1, EXPERT_FF, D), lambda g, r, e: (e, 0, 0)),
                  vec, vec],
        out_specs=blk,
        scratch_shapes=[pltpu.VMEM((gb * tr, LANE), F32), pltpu.VMEM((gb * tr, D), F32)],
        compiler_params=_cparams(("parallel", "parallel", "arbitrary")),
        name="moe",
    )(h3, x3, mod, wrg, brg, wre, bre, wgu, wd, ln_g.reshape(1, 1, D), ln_b.reshape(1, 1, D))


def _head_slots(w, n_heads, width, lane_off):
    lead = w.shape[:-1]
    parts = []
    for h in range(n_heads):
        piece = w[..., h * width:(h + 1) * width]
        parts.append(jnp.pad(piece, [(0, 0)] * len(lead) + [(lane_off[h], LANE - width - lane_off[h])]))
    return jnp.concatenate(parts, axis=-1)


def _prep_w_in(w_in):
    wb = w_in.astype(BF16)
    o = 0
    pieces = {}
    for name, wd in (("q", 512), ("k", 128), ("v", 128), ("glu", 1024), ("cq", 512), ("ckv", 256), ("kr", 32),
                     ("pool", 512), ("gates", 4 * D_MODEL)):
        pieces[name] = wb[..., o:o + wd]
        o += wd
    q_off = [0 if h < SB_GROUP else HEAD_DIM for h in range(SB_HEADS)]
    kv_off = [0, HEAD_DIM]
    kr = pieces["kr"]
    half = MLA_ROPE // 2
    kr_sw = jnp.concatenate([kr[..., half:], kr[..., :half]], axis=-1)
    small = jnp.concatenate([
        _head_slots(pieces["q"], SB_HEADS, HEAD_DIM, q_off),
        _head_slots(pieces["k"], SB_KV_HEADS, HEAD_DIM, kv_off),
        _head_slots(pieces["v"], SB_KV_HEADS, HEAD_DIM, kv_off),
        pieces["glu"], pieces["cq"], pieces["ckv"],
        _head_slots(kr, 1, MLA_ROPE, [0]), _head_slots(kr_sw, 1, MLA_ROPE, [0]),
        pieces["pool"]], axis=-1)
    return small, pieces["gates"]


def _prep_branch(w_branch):
    wb = w_branch.astype(BF16)
    q_off = [0 if h < SB_GROUP else HEAD_DIM for h in range(SB_HEADS)]
    wa = jnp.swapaxes(_head_slots(jnp.swapaxes(wb[:, 0], 1, 2), SB_HEADS, HEAD_DIM, q_off), 1, 2)
    wc = jnp.swapaxes(_head_slots(jnp.swapaxes(wb[:, 2], 1, 2), MLA_HEADS, HEAD_DIM, [0] * MLA_HEADS), 1, 2)
    return wa, wb[:, 1], wc, wb[:, 3]


def _prep_mla(w_uq, w_uk, w_uv):
    L = w_uq.shape[0]
    wq = w_uq.astype(BF16)
    half = MLA_ROPE // 2
    nope = wq[..., :MLA_NOPE].reshape(L, MLA_Q_LORA, MLA_HEADS * MLA_NOPE)
    rp = wq[..., MLA_NOPE:]
    rp_sw = jnp.concatenate([rp[..., half:], rp[..., :half]], axis=-1)
    rp = rp.reshape(L, MLA_Q_LORA, MLA_HEADS * MLA_ROPE)
    rp_sw = rp_sw.reshape(L, MLA_Q_LORA, MLA_HEADS * MLA_ROPE)
    z8 = [0] * MLA_HEADS
    wq3 = jnp.concatenate([_head_slots(nope, MLA_HEADS, MLA_NOPE, z8), _head_slots(rp, MLA_HEADS, MLA_ROPE, z8),
                           _head_slots(rp_sw, MLA_HEADS, MLA_ROPE, z8)], axis=-1)
    wukT = jnp.transpose(w_uk.astype(BF16), (0, 2, 3, 1))
    wukT = jnp.pad(wukT, ((0, 0), (0, 0), (0, LANE - MLA_NOPE), (0, 0)))
    wuv = jnp.transpose(w_uv.astype(BF16), (0, 2, 1, 3))
    wuv = jnp.pad(wuv, ((0, 0), (0, 0), (0, 0), (0, LANE - HEAD_DIM)))
    return wq3, wukT, wuv


def _rope_tables(pos):
    half = MLA_ROPE // 2
    inv_freq = ROPE_THETA ** (-jnp.arange(half, dtype=F32) / half)
    ang = pos.astype(F32)[:, None] * inv_freq
    cos, sin = jnp.cos(ang), jnp.sin(ang)
    pad = ((0, 0), (0, LANE - MLA_ROPE))
    return jnp.pad(jnp.concatenate([cos, cos], -1), pad), jnp.pad(jnp.concatenate([-sin, sin], -1), pad)


def _unslot(z, base, n_heads, lane_off):
    return jnp.stack([z[..., base + h * LANE + lane_off[h]:base + h * LANE + lane_off[h] + HEAD_DIM]
                      for h in range(n_heads)], axis=-2)


def kernel(x_prompt, x_sample, cache_sb_k, cache_sb_v, cache_mla_latent, cache_mla_krope, state_conv, state_pool, page_table, c_prompt, c_sample, w_ada, b_ada, w_in, conv_w, conv_b, conv_ln_g, conv_ln_b, mla_q_norm, w_uq, mla_kv_norm, w_uk, w_uv, pool_w, pool_scale, w_branch, w_out, ln1_g, ln1_b, w_router_grp, b_router_grp, w_router_exp, b_router_exp, w_exp_gate, w_exp_up, w_exp_down, ln2_g, ln2_b):
    B, T, D = x_prompt.shape
    DB, TS, _ = x_sample.shape
    L = w_ada.shape[0]
    n_pool, page = cache_sb_k.shape[1], cache_sb_k.shape[2]
    past_len = page_table.shape[1] * page
    kv_off = [0, HEAD_DIM]

    w_small, w_gates = _prep_w_in(w_in)
    wa, wb_, wc, wd_ = _prep_branch(w_branch)
    wq3, wukT, wuv = _prep_mla(w_uq, w_uk, w_uv)
    w_out_b = w_out.astype(BF16)
    pool_w_b = pool_w.astype(BF16)
    wrg = jnp.pad(w_router_grp.astype(BF16), ((0, 0), (0, 0), (0, LANE - N_GROUPS)))
    brg = jnp.pad(b_router_grp, ((0, 0), (0, LANE - N_GROUPS)))
    wre = jnp.pad(w_router_exp.astype(BF16).reshape(L, D, N_EXPERTS), ((0, 0), (0, 0), (0, LANE - N_EXPERTS)))
    bre = jnp.pad(b_router_exp.reshape(L, N_EXPERTS), ((0, 0), (0, LANE - N_EXPERTS)))
    wgu = jnp.concatenate([w_exp_gate.astype(BF16), w_exp_up.astype(BF16)], axis=-1)
    wdn = w_exp_down.astype(BF16)
    cache_k4 = cache_sb_k.reshape(L, n_pool, page, SB_KV_HEADS * HEAD_DIM)
    cache_v4 = cache_sb_v.reshape(L, n_pool, page, SB_KV_HEADS * HEAD_DIM)

    cos_p, sin_p = _rope_tables(jnp.arange(T))
    cos_p, sin_p = jnp.tile(cos_p, (B, 1)), jnp.tile(sin_p, (B, 1))
    cos_s, sin_s = _rope_tables(past_len + jnp.arange(TS))
    cos_s, sin_s = jnp.repeat(cos_s, DB, axis=0), jnp.repeat(sin_s, DB, axis=0)

    n_c = B + DB
    n_c_pad = -(-n_c // 8) * 8
    c_all = jnp.pad(jnp.concatenate([c_prompt, c_sample], axis=0), ((0, n_c_pad - n_c), (0, 0)))
    mod_all = _ada(c_all, w_ada, b_ada)

    xp = x_prompt
    xs = jnp.swapaxes(x_sample, 0, 1)
    rows_p = [[] for _ in range(6)]
    rows_s = [[] for _ in range(6)]
    for l in range(L):
        mod_p = mod_all[l, :B][:, None, :]
        mod_s = mod_all[l, B:B + DB][None]

        zp = _lnmm(xp, mod_p, 1, 0, w_small[l], None, F32, 512)
        gp = _lnmm(xp, mod_p, 1, 0, w_gates[l], "sigmoid", BF16, 1024)
        oa_p = _sb_prompt(zp)
        qcat_p, lat_p, kro_p = _mla_prep(zp.reshape(B * T, Z_SMALL), cos_p, sin_p, mla_q_norm[l][None],
                                         mla_kv_norm[l][None], wq3[l], wukT[l])
        oc_p = _mla_prompt(qcat_p.reshape(B, T, -1), lat_p.reshape(B, T, -1), kro_p.reshape(B, T, -1), wuv[l])
        ob_p, od_p, tail_p = _mix_prompt(zp, conv_w[l], conv_b[l][None], conv_ln_g[l][None], conv_ln_b[l][None],
                                         pool_w_b[l], pool_scale[l][None])
        mg_p = _merge(oa_p.reshape(B * T, -1), ob_p.reshape(B * T, -1), oc_p.reshape(B * T, -1),
                      od_p.reshape(B * T, -1), gp.reshape(B * T, -1), wa[l], wb_[l], wc[l], wd_[l])
        x1_p, h2_p = _out_proj(mg_p.reshape(B, T, D), xp, mod_p, w_out_b[l], ln1_g[l], ln1_b[l])
        xp = _moe(h2_p, x1_p, mod_p, wrg[l], brg[l][None], wre[l], bre[l][None], wgu[l], wdn[l], ln2_g[l], ln2_b[l])

        rows_p[0].append(_unslot(zp, ZC_K, SB_KV_HEADS, kv_off))
        rows_p[1].append(_unslot(zp, ZC_V, SB_KV_HEADS, kv_off))
        rows_p[2].append(lat_p.reshape(B, T, MLA_KV_LORA))
        rows_p[3].append(kro_p.reshape(B, T, LANE)[..., :MLA_ROPE])
        rows_p[4].append(tail_p[:, HALO - (CONV_WIDTH - 1):])
        rows_p[5].append(zp[:, T - POOL_STATE:, ZC_POOL:ZC_POOL + BRANCH_W])

        zs = _lnmm(xs, mod_s, 1, 0, w_small[l], None, F32, 512)
        gs = _lnmm(xs, mod_s, 1, 0, w_gates[l], "sigmoid", BF16, 1024)
        zs2 = zs.reshape(TS * DB, Z_SMALL)
        qcat_s, lat_s, kro_s = _mla_prep(zs2, cos_s, sin_s, mla_q_norm[l][None], mla_kv_norm[l][None], wq3[l], wukT[l])
        conv_tm = jnp.swapaxes(state_conv[l], 0, 1).reshape((CONV_WIDTH - 1) * DB, BRANCH_W)
        pool_tm = jnp.swapaxes(state_pool[l], 0, 1).reshape(POOL_STATE * DB, BRANCH_W)
        ob_s, od_s, u_s = _mix_sample(zs2, conv_tm, pool_tm, DB, past_len, conv_w[l], conv_b[l][None],
                                      conv_ln_g[l][None], conv_ln_b[l][None], pool_w_b[l], pool_scale[l][None])

        def bm(a, heads):
            w_ = a.shape[-1] // heads
            return jnp.transpose(a.reshape(TS, DB, heads, w_), (1, 2, 0, 3)).reshape(DB, heads * TS, w_)

        def pad_keys(a):
            a = jnp.swapaxes(a, 0, 1)
            return jnp.pad(a, ((0, 0), (0, LANE - TS), (0, 0)))

        k_new = pad_keys(zs[..., ZC_K:ZC_K + LANE] + zs[..., ZC_K + LANE:ZC_K + 2 * LANE])
        v_new = pad_keys(zs[..., ZC_V:ZC_V + LANE] + zs[..., ZC_V + LANE:ZC_V + 2 * LANE])
        lat_new = pad_keys(lat_s.reshape(TS, DB, MLA_KV_LORA))
        kr_new = pad_keys(kro_s.reshape(TS, DB, LANE)[..., :MLA_ROPE])
        oa_b, oc_b = _dec_attention(l, page_table, bm(zs2[:, ZC_Q:ZC_Q + SB_HEADS * LANE], SB_HEADS),
                                    bm(qcat_s, MLA_HEADS), k_new, v_new, lat_new, kr_new,
                                    cache_k4, cache_v4, cache_mla_latent, cache_mla_krope, wuv[l], TS)

        def tm(a, heads):
            return jnp.transpose(a.reshape(DB, heads, TS, LANE), (2, 0, 1, 3)).reshape(TS * DB, heads * LANE)

        mg_s = _merge(tm(oa_b, SB_HEADS), ob_s, tm(oc_b, MLA_HEADS), od_s, gs.reshape(TS * DB, -1),
                      wa[l], wb_[l], wc[l], wd_[l])
        x1_s, h2_s = _out_proj(mg_s.reshape(TS, DB, D), xs, mod_s, w_out_b[l], ln1_g[l], ln1_b[l])
        xs = _moe(h2_s, x1_s, mod_s, wrg[l], brg[l][None], wre[l], bre[l][None], wgu[l], wdn[l], ln2_g[l], ln2_b[l])

        zs_b = jnp.swapaxes(zs, 0, 1)
        rows_s[0].append(_unslot(zs_b, ZC_K, SB_KV_HEADS, kv_off))
        rows_s[1].append(_unslot(zs_b, ZC_V, SB_KV_HEADS, kv_off))
        rows_s[2].append(jnp.swapaxes(lat_s.reshape(TS, DB, MLA_KV_LORA), 0, 1))
        rows_s[3].append(jnp.swapaxes(kro_s.reshape(TS, DB, LANE)[..., :MLA_ROPE], 0, 1))
        u_b = jnp.swapaxes(u_s.reshape(TS, DB, BRANCH_W), 0, 1)
        rows_s[4].append(jnp.concatenate([state_conv[l], u_b], axis=1)[:, -(CONV_WIDTH - 1):])
        rows_s[5].append(jnp.concatenate([state_pool[l], zs_b[..., ZC_POOL:ZC_POOL + BRANCH_W]], axis=1)[:, -POOL_STATE:])

    outs_p = [jnp.stack(r) for r in rows_p]
    outs_s = [jnp.stack(r) for r in rows_s]
    return (xp, jnp.swapaxes(xs, 0, 1), *outs_p, *outs_s)
```

```python
import functools

import jax
import jax.numpy as jnp
from jax import lax
from jax.experimental import pallas as pl
from jax.experimental.pallas import tpu as pltpu

F32 = jnp.float32
BF16 = jnp.bfloat16

D_MODEL = 2048
BRANCH_W = 512
HEAD_DIM = 64
SB_HEADS = 8
SB_KV_HEADS = 2
SB_GROUP = 4
CONV_WIDTH = 31
MLA_HEADS = 8
MLA_NOPE = 64
MLA_ROPE = 32
MLA_Q_LORA = 512
MLA_KV_LORA = 256
MLA_SCALE = (MLA_NOPE + MLA_ROPE) ** -0.5
ROPE_THETA = 10000.0
POOL_WINDOWS = (2, 4, 8, 16)
POOL_GROUP = 128
POOL_STATE = 15
N_GROUPS = 4
EXPERTS_PER_GROUP = 8
N_EXPERTS = 32
EXPERT_FF = 256
LN_EPS = 1e-5
RMS_EPS = 1e-6
DEPTH_FOR_DEEPNORM = 4
DEEPNORM_ALPHA = (2 * DEPTH_FOR_DEEPNORM) ** 0.25
NEG_BIG = -1e30

LANE = 128
HALO = 32
VMEM_LIMIT = 56 * 1024 * 1024

ZC_Q = 0
ZC_K = 1024
ZC_V = 1280
ZC_GLU = 1536
ZC_CQ = 2560
ZC_CKV = 3072
ZC_KR = 3328
ZC_KRS = 3456
ZC_POOL = 3584
Z_SMALL = 4096

TM_ROWS = 512
TQ_SB = 256
TK_SB = 256
TQ_MLA = 256
TK_MLA = 256
TT_MIX = 512
DEC_PAGES_PER_STEP = 16


def _cparams(sem):
    return pltpu.CompilerParams(dimension_semantics=sem, vmem_limit_bytes=VMEM_LIMIT)


def _ln(x):
    mu = jnp.mean(x, axis=-1, keepdims=True)
    xc = x - mu
    var = jnp.mean(xc * xc, axis=-1, keepdims=True)
    return xc * lax.rsqrt(var + LN_EPS)


def _softplus(z):
    return jnp.maximum(z, 0.0) + jnp.log1p(jnp.exp(-jnp.abs(z)))


def _dot_nt(a, b):
    return lax.dot_general(a, b, (((1,), (1,)), ((), ())), preferred_element_type=F32)


def _ada_kernel(c_ref, w_ref, b_ref, o_ref):
    c = c_ref[...]
    s = (c * jax.nn.sigmoid(c)).astype(BF16)
    o_ref[0] = jnp.dot(s, w_ref[0].astype(BF16), preferred_element_type=F32) + b_ref[0]


def _ada(c_all, w_ada, b_ada):
    L, D, N = w_ada.shape
    M = c_all.shape[0]
    tn = 1024
    return pl.pallas_call(
        _ada_kernel,
        out_shape=jax.ShapeDtypeStruct((L, M, N), F32),
        grid=(L, N // tn),
        in_specs=[
            pl.BlockSpec((M, D), lambda l, n: (0, 0)),
            pl.BlockSpec((1, D, tn), lambda l, n: (l, 0, n)),
            pl.BlockSpec((1, 1, tn), lambda l, n: (l, 0, n)),
        ],
        out_specs=pl.BlockSpec((1, M, tn), lambda l, n: (l, 0, n)),
        compiler_params=_cparams(("parallel", "parallel")),
        name="ada",
    )(c_all, w_ada, b_ada.reshape(L, 1, N))


def _mod_spec(mod, chunk, tr, nd):
    gm, rm, _ = mod.shape
    if rm == 1:
        if nd == 3:
            return pl.BlockSpec((1, 1, D_MODEL), lambda g, r, n: (g, 0, chunk))
        return pl.BlockSpec((1, 1, D_MODEL), lambda g, r: (g, 0, chunk))
    if nd == 3:
        return pl.BlockSpec((1, tr, D_MODEL), lambda g, r, n: (0, r, chunk))
    return pl.BlockSpec((1, tr, D_MODEL), lambda g, r: (0, r, chunk))


def _row_tiling(x3, mod):
    G, R, _ = x3.shape
    if mod.shape[1] == 1:
        return 1, min(TM_ROWS, R)
    return G, R


def _lnmm_kernel(x_ref, sc_ref, sh_ref, w_ref, o_ref, h_scr, *, act):
    @pl.when(pl.program_id(2) == 0)
    def _():
        h = _ln(x_ref[...]) * (1.0 + sc_ref[...]) + sh_ref[...]
        h_scr[...] = h.reshape(h_scr.shape).astype(BF16)

    y = jnp.dot(h_scr[...], w_ref[...], preferred_element_type=F32)
    if act == "sigmoid":
        y = jax.nn.sigmoid(y)
    o_ref[...] = y.reshape(o_ref.shape).astype(o_ref.dtype)


def _lnmm(x3, mod, sc_chunk, sh_chunk, w, act, out_dtype, tn):
    G, R, D = x3.shape
    N = w.shape[1]
    gb, tr = _row_tiling(x3, mod)
    return pl.pallas_call(
        functools.partial(_lnmm_kernel, act=act),
        out_shape=jax.ShapeDtypeStruct((G, R, N), out_dtype),
        grid=(G // gb, R // tr, N // tn),
        in_specs=[
            pl.BlockSpec((gb, tr, D), lambda g, r, n: (g, r, 0)),
            _mod_spec(mod, sc_chunk, tr, 3),
            _mod_spec(mod, sh_chunk, tr, 3),
            pl.BlockSpec((D, tn), lambda g, r, n: (0, n)),
        ],
        out_specs=pl.BlockSpec((gb, tr, tn), lambda g, r, n: (g, r, n)),
        scratch_shapes=[pltpu.VMEM((gb * tr, D), BF16)],
        compiler_params=_cparams(("parallel", "parallel", "arbitrary")),
        name="lnmm_" + str(act),
    )(x3, mod, mod, w)


def _sb_tri(tk):
    j = lax.broadcasted_iota(jnp.int32, (tk, tk), 0)
    s = lax.broadcasted_iota(jnp.int32, (tk, tk), 1)
    u = (j > s).astype(BF16)
    return jnp.concatenate([u, jnp.ones((tk, LANE), BF16)], axis=1)


def _sb_block(qs, kb, vb, u2, mask, carry, tk):
    m = qs.shape[0]
    z = _dot_nt(qs, kb)
    sp = _softplus(z)
    lk = -sp if mask is None else jnp.where(mask, -sp, 0.0)
    hi = lk.astype(BF16)
    lo = (lk - hi.astype(F32)).astype(BF16)
    r = jnp.dot(jnp.concatenate([hi, lo], axis=0), u2, preferred_element_type=F32)
    r = r[:m] + r[m:]
    tail = r[:, :tk] + jnp.concatenate([carry] * (tk // LANE), axis=1)
    a = jnp.exp(z - sp + tail)
    if mask is not None:
        a = jnp.where(mask, a, 0.0)
    pv = jnp.dot(a.astype(BF16), vb, preferred_element_type=F32)
    return pv, carry + r[:, tk:]


def _sbp_kernel(q_ref, k_ref, v_ref, u2_ref, o_ref, acc_scr, car_scr, *, tq, tk):
    i = pl.program_id(2)
    q = q_ref[0]
    qs = jnp.concatenate([q[:, g * LANE:(g + 1) * LANE] for g in range(SB_GROUP)], axis=0)
    qs = (qs * (HEAD_DIM ** -0.5)).astype(BF16)
    acc_scr[...] = jnp.zeros_like(acc_scr)
    car_scr[...] = jnp.zeros_like(car_scr)
    nkb = (i * tq + tq - 1) // tk + 1
    row = lax.broadcasted_iota(jnp.int32, (tq, tk), 0)
    qpos = i * tq + jnp.concatenate([row] * SB_GROUP, axis=0)
    col = lax.broadcasted_iota(jnp.int32, (SB_GROUP * tq, tk), 1)
    u2 = u2_ref[...]

    def body(jj, c):
        ks = pl.multiple_of((nkb - 1 - jj) * tk, tk)
        kb = k_ref[0, pl.ds(ks, tk), :].astype(BF16)
        vb = v_ref[0, pl.ds(ks, tk), :].astype(BF16)
        mask = (ks + col) < qpos
        pv, car = _sb_block(qs, kb, vb, u2, mask, car_scr[...], tk)
        acc_scr[...] += pv
        car_scr[...] = car
        return c

    lax.fori_loop(0, nkb, body, 0)
    acc = acc_scr[...]
    for g in range(SB_GROUP):
        o_ref[0, :, g * LANE:(g + 1) * LANE] = acc[g * tq:(g + 1) * tq].astype(o_ref.dtype)


def _sb_prompt(z3):
    B, T, _ = z3.shape
    tq = min(TQ_SB, T)
    tk = min(TK_SB, T)
    qw = SB_GROUP * LANE
    return pl.pallas_call(
        functools.partial(_sbp_kernel, tq=tq, tk=tk),
        out_shape=jax.ShapeDtypeStruct((B, T, SB_HEADS * LANE), BF16),
        grid=(B, SB_KV_HEADS, T // tq),
        in_specs=[
            pl.BlockSpec((1, tq, qw), lambda b, h, i: (b, i, ZC_Q // qw + h)),
            pl.BlockSpec((1, T, LANE), lambda b, h, i: (b, 0, ZC_K // LANE + h)),
            pl.BlockSpec((1, T, LANE), lambda b, h, i: (b, 0, ZC_V // LANE + h)),
            pl.BlockSpec((tk, tk + LANE), lambda b, h, i: (0, 0)),
        ],
        out_specs=pl.BlockSpec((1, tq, qw), lambda b, h, i: (b, i, h)),
        scratch_shapes=[pltpu.VMEM((SB_GROUP * tq, LANE), F32), pltpu.VMEM((SB_GROUP * tq, LANE), F32)],
        compiler_params=_cparams(("parallel", "parallel", "parallel")),
        name="sb_prompt",
    )(z3, z3, z3, _sb_tri(tk))


def _mlaprep_kernel(cq_ref, ckv_ref, kr_ref, krs_ref, ct_ref, st_ref, gq_ref, gkv_ref, wq_ref, wuk_ref,
                    qcat_ref, lat_ref, kro_ref):
    cq = cq_ref[...]
    cqn = (cq * lax.rsqrt(jnp.mean(cq * cq, axis=-1, keepdims=True) + RMS_EPS)) * gq_ref[...]
    q3 = jnp.dot(cqn.astype(BF16), wq_ref[...], preferred_element_type=F32)
    ct = ct_ref[...]
    st = st_ref[...]
    hw = MLA_HEADS * LANE
    qw = MLA_KV_LORA + LANE
    for h in range(MLA_HEADS):
        qn = q3[:, h * LANE:(h + 1) * LANE].astype(BF16)
        ql = jnp.dot(qn, wuk_ref[h], preferred_element_type=F32)
        qr = q3[:, hw + h * LANE:hw + (h + 1) * LANE] * ct + q3[:, 2 * hw + h * LANE:2 * hw + (h + 1) * LANE] * st
        qcat_ref[:, h * qw:h * qw + MLA_KV_LORA] = ql.astype(BF16)
        qcat_ref[:, h * qw + MLA_KV_LORA:(h + 1) * qw] = qr.astype(BF16)
    ckv = ckv_ref[...]
    lat_ref[...] = (ckv * lax.rsqrt(jnp.mean(ckv * ckv, axis=-1, keepdims=True) + RMS_EPS)) * gkv_ref[...]
    kro_ref[...] = kr_ref[...] * ct + krs_ref[...] * st


def _mla_prep(z2, cos_t, sin_t, gq, gkv, wq3, wukT):
    M = z2.shape[0]
    tm = min(TM_ROWS, M)
    qw = MLA_KV_LORA + LANE
    return pl.pallas_call(
        _mlaprep_kernel,
        out_shape=(jax.ShapeDtypeStruct((M, MLA_HEADS * qw), BF16),
                   jax.ShapeDtypeStruct((M, MLA_KV_LORA), F32),
                   jax.ShapeDtypeStruct((M, LANE), F32)),
        grid=(M // tm,),
        in_specs=[
            pl.BlockSpec((tm, MLA_Q_LORA), lambda i: (i, ZC_CQ // MLA_Q_LORA)),
            pl.BlockSpec((tm, MLA_KV_LORA), lambda i: (i, ZC_CKV // MLA_KV_LORA)),
            pl.BlockSpec((tm, LANE), lambda i: (i, ZC_KR // LANE)),
            pl.BlockSpec((tm, LANE), lambda i: (i, ZC_KRS // LANE)),
            pl.BlockSpec((tm, LANE), lambda i: (i, 0)),
            pl.BlockSpec((tm, LANE), lambda i: (i, 0)),
            pl.BlockSpec((1, MLA_Q_LORA), lambda i: (0, 0)),
            pl.BlockSpec((1, MLA_KV_LORA), lambda i: (0, 0)),
            pl.BlockSpec((MLA_Q_LORA, 3 * MLA_HEADS * LANE), lambda i: (0, 0)),
            pl.BlockSpec((MLA_HEADS, LANE, MLA_KV_LORA), lambda i: (0, 0, 0)),
        ],
        out_specs=(pl.BlockSpec((tm, MLA_HEADS * qw), lambda i: (i, 0)),
                   pl.BlockSpec((tm, MLA_KV_LORA), lambda i: (i, 0)),
                   pl.BlockSpec((tm, LANE), lambda i: (i, 0))),
        compiler_params=_cparams(("parallel",)),
        name="mla_prep",
    )(z2, z2, z2, z2, cos_t, sin_t, gq, gkv, wq3, wukT)


def _mla_update(qh, kcat_t, mask, m_prev, l_prev, acc_prev, latb):
    s = (_dot_nt(qh[:, :MLA_KV_LORA], kcat_t[0]) +
         _dot_nt(qh[:, MLA_KV_LORA:MLA_KV_LORA + kcat_t[1].shape[1]], kcat_t[1])) * MLA_SCALE
    if mask is not None:
        s = jnp.where(mask, s, NEG_BIG)
    m_new = jnp.maximum(m_prev, jnp.max(s, axis=-1, keepdims=True))
    p = jnp.exp(s - m_new)
    alpha = jnp.exp(m_prev - m_new)
    l_new = alpha * l_prev + jnp.sum(p, axis=-1, keepdims=True)
    acc_new = alpha * acc_prev + jnp.dot(p.astype(BF16), latb, preferred_element_type=F32)
    return m_new, l_new, acc_new


def _mlap_kernel(q_ref, lat_ref, kr_ref, wuv_ref, o_ref, m_scr, l_scr, acc_scr, *, tq, tk):
    i = pl.program_id(1)
    m_scr[...] = jnp.full_like(m_scr, NEG_BIG)
    l_scr[...] = jnp.zeros_like(l_scr)
    acc_scr[...] = jnp.zeros_like(acc_scr)
    nkb = (i * tq + tq - 1) // tk + 1
    qpos = i * tq + lax.broadcasted_iota(jnp.int32, (tq, tk), 0)
    col = lax.broadcasted_iota(jnp.int32, (tq, tk), 1)
    qw = MLA_KV_LORA + LANE

    def body(j, c):
        ks = pl.multiple_of(j * tk, tk)
        latb = lat_ref[0, pl.ds(ks, tk), :].astype(BF16)
        krb = kr_ref[0, pl.ds(ks, tk), :].astype(BF16)
        mask = (ks + col) <= qpos
        for h in range(MLA_HEADS):
            qh = q_ref[0, :, h * qw:(h + 1) * qw]
            m_new, l_new, acc_new = _mla_update(qh, (latb, krb), mask, m_scr[h], l_scr[h], acc_scr[h], latb)
            m_scr[h] = m_new
            l_scr[h] = l_new
            acc_scr[h] = acc_new
        return c

    lax.fori_loop(0, nkb, body, 0)
    for h in range(MLA_HEADS):
        ol = (acc_scr[h] / l_scr[h]).astype(BF16)
        o_ref[0, :, h * LANE:(h + 1) * LANE] = jnp.dot(ol, wuv_ref[h], preferred_element_type=F32).astype(o_ref.dtype)


def _mla_prompt(qcat3, lat3, kro3, wuv):
    B, T, QW = qcat3.shape
    tq = min(TQ_MLA, T)
    tk = min(TK_MLA, T)
    return pl.pallas_call(
        functools.partial(_mlap_kernel, tq=tq, tk=tk),
        out_shape=jax.ShapeDtypeStruct((B, T, MLA_HEADS * LANE), BF16),
        grid=(B, T // tq),
        in_specs=[
            pl.BlockSpec((1, tq, QW), lambda b, i: (b, i, 0)),
            pl.BlockSpec((1, T, MLA_KV_LORA), lambda b, i: (b, 0, 0)),
            pl.BlockSpec((1, T, LANE), lambda b, i: (b, 0, 0)),
            pl.BlockSpec((MLA_HEADS, MLA_KV_LORA, LANE), lambda b, i: (0, 0, 0)),
        ],
        out_specs=pl.BlockSpec((1, tq, MLA_HEADS * LANE), lambda b, i: (b, i, 0)),
        scratch_shapes=[pltpu.VMEM((MLA_HEADS, tq, 1), F32), pltpu.VMEM((MLA_HEADS, tq, 1), F32),
                        pltpu.VMEM((MLA_HEADS, tq, MLA_KV_LORA), F32)],
        compiler_params=_cparams(("parallel", "parallel")),
        name="mla_prompt",
    )(qcat3, lat3, kro3, wuv)


def _conv_pool_core(ext_c, ext_p, rows, bn, hc, hp, cw_ref, cb_ref, lg_ref, lb_ref, pw_ref, ps_ref, pos,
                    ob_ref, od_ref):
    y = jnp.zeros((rows, BRANCH_W), F32) + cb_ref[...]
    for k in range(CONV_WIDTH):
        y = y + cw_ref[k:k + 1, :] * ext_c[pl.ds(hc - (CONV_WIDTH - 1 - k) * bn, rows), :]
    y = _ln(y) * lg_ref[...] + lb_ref[...]
    ob_ref[...] = (y * jax.nn.sigmoid(y)).reshape(ob_ref.shape).astype(ob_ref.dtype)
    outs = []
    for g, w in enumerate(POOL_WINDOWS):
        sl = slice(g * POOL_GROUP, (g + 1) * POOL_GROUP)
        x0 = ext_p[pl.ds(hp, rows), sl]
        ws = x0
        for d in range(1, w):
            ws = ws + ext_p[pl.ds(hp - d * bn, rows), sl]
        cnt = jnp.minimum(pos + 1, w).astype(F32)
        diff = ws / cnt - x0
        mixed = jnp.dot(diff.astype(BF16), pw_ref[g], preferred_element_type=F32) * ps_ref[:, sl]
        outs.append(mixed)
    od_ref[...] = jnp.concatenate(outs, axis=1).reshape(od_ref.shape).astype(od_ref.dtype)


def _mixp_kernel(ga_ref, gb_ref, gah_ref, gbh_ref, pu_ref, puh_ref, cw_ref, cb_ref, lg_ref, lb_ref, pw_ref, ps_ref,
                 ob_ref, od_ref, tail_ref, ext_c, ext_p, *, tt):
    i = pl.program_id(1)
    keep = (i > 0).astype(F32)
    ext_c[0:HALO, :] = gah_ref[0] * jax.nn.sigmoid(gbh_ref[0]) * keep
    ext_c[HALO:HALO + tt, :] = ga_ref[0] * jax.nn.sigmoid(gb_ref[0])
    ext_p[0:HALO, :] = puh_ref[0] * keep
    ext_p[HALO:HALO + tt, :] = pu_ref[0]
    pos = i * tt + lax.broadcasted_iota(jnp.int32, (tt, 1), 0)
    _conv_pool_core(ext_c, ext_p, tt, 1, HALO, HALO, cw_ref, cb_ref, lg_ref, lb_ref, pw_ref, ps_ref, pos,
                    ob_ref, od_ref)
    tail_ref[0] = ext_c[tt:tt + HALO, :]


def _mix_prompt(z3, cw, cb, lg, lb, pw, ps):
    B, T, _ = z3.shape
    tt = min(TT_MIX, T)
    hb = tt // HALO
    w = BRANCH_W

    def halo_map(col):
        return lambda b, i: (b, jnp.maximum(i * hb - 1, 0), col)

    const2 = lambda b, i: (0, 0)
    return pl.pallas_call(
        functools.partial(_mixp_kernel, tt=tt),
        out_shape=(jax.ShapeDtypeStruct((B, T, w), BF16), jax.ShapeDtypeStruct((B, T, w), BF16),
                   jax.ShapeDtypeStruct((B, HALO, w), F32)),
        grid=(B, T // tt),
        in_specs=[
            pl.BlockSpec((1, tt, w), lambda b, i: (b, i, ZC_GLU // w)),
            pl.BlockSpec((1, tt, w), lambda b, i: (b, i, ZC_GLU // w + 1)),
            pl.BlockSpec((1, HALO, w), halo_map(ZC_GLU // w)),
            pl.BlockSpec((1, HALO, w), halo_map(ZC_GLU // w + 1)),
            pl.BlockSpec((1, tt, w), lambda b, i: (b, i, ZC_POOL // w)),
            pl.BlockSpec((1, HALO, w), halo_map(ZC_POOL // w)),
            pl.BlockSpec((CONV_WIDTH, w), const2),
            pl.BlockSpec((1, w), const2),
            pl.BlockSpec((1, w), const2),
            pl.BlockSpec((1, w), const2),
            pl.BlockSpec((len(POOL_WINDOWS), POOL_GROUP, POOL_GROUP), lambda b, i: (0, 0, 0)),
            pl.BlockSpec((1, w), const2),
        ],
        out_specs=(pl.BlockSpec((1, tt, w), lambda b, i: (b, i, 0)),
                   pl.BlockSpec((1, tt, w), lambda b, i: (b, i, 0)),
                   pl.BlockSpec((1, HALO, w), lambda b, i: (b, 0, 0))),
        scratch_shapes=[pltpu.VMEM((HALO + tt, w), F32), pltpu.VMEM((HALO + tt, w), F32)],
        compiler_params=_cparams(("parallel", "arbitrary")),
        name="mix_prompt",
    )(z3, z3, z3, z3, z3, z3, cw, cb, lg, lb, pw, ps)


def _mixs_kernel(ga_ref, gb_ref, sc_ref, pu_ref, sp_ref, cw_ref, cb_ref, lg_ref, lb_ref, pw_ref, ps_ref,
                 ob_ref, od_ref, u_ref, ext_c, ext_p, *, rows, bn, pos0):
    hc = sc_ref.shape[0]
    hp = sp_ref.shape[0]
    u = ga_ref[...] * jax.nn.sigmoid(gb_ref[...])
    u_ref[...] = u
    ext_c[0:hc, :] = sc_ref[...]
    ext_c[hc:hc + rows, :] = u
    ext_p[0:hp, :] = sp_ref[...]
    ext_p[hp:hp + rows, :] = pu_ref[...]
    pos = pos0 + lax.broadcasted_iota(jnp.int32, (rows, 1), 0) // bn
    _conv_pool_core(ext_c, ext_p, rows, bn, hc, hp, cw_ref, cb_ref, lg_ref, lb_ref, pw_ref, ps_ref, pos,
                    ob_ref, od_ref)


def _mix_sample(z2, conv_state_tm, pool_state_tm, bn, pos0, cw, cb, lg, lb, pw, ps):
    rows = z2.shape[0]
    w = BRANCH_W
    hc = conv_state_tm.shape[0]
    hp = pool_state_tm.shape[0]
    const2 = lambda i: (0, 0)
    return pl.pallas_call(
        functools.partial(_mixs_kernel, rows=rows, bn=bn, pos0=pos0),
        out_shape=(jax.ShapeDtypeStruct((rows, w), BF16), jax.ShapeDtypeStruct((rows, w), BF16),
                   jax.ShapeDtypeStruct((rows, w), F32)),
        grid=(1,),
        in_specs=[
            pl.BlockSpec((rows, w), lambda i: (0, ZC_GLU // w)),
            pl.BlockSpec((rows, w), lambda i: (0, ZC_GLU // w + 1)),
            pl.BlockSpec((hc, w), const2),
            pl.BlockSpec((rows, w), lambda i: (0, ZC_POOL // w)),
            pl.BlockSpec((hp, w), const2),
            pl.BlockSpec((CONV_WIDTH, w), const2),
            pl.BlockSpec((1, w), const2),
            pl.BlockSpec((1, w), const2),
            pl.BlockSpec((1, w), const2),
            pl.BlockSpec((len(POOL_WINDOWS), POOL_GROUP, POOL_GROUP), lambda i: (0, 0, 0)),
            pl.BlockSpec((1, w), const2),
        ],
        out_specs=(pl.BlockSpec((rows, w), const2), pl.BlockSpec((rows, w), const2), pl.BlockSpec((rows, w), const2)),
        scratch_shapes=[pltpu.VMEM((hc + rows, w), F32), pltpu.VMEM((hp + rows, w), F32)],
        compiler_params=_cparams(("arbitrary",)),
        name="mix_sample",
    )(z2, z2, conv_state_tm, z2, pool_state_tm, cw, cb, lg, lb, pw, ps)


def _dec_sb(qs, kt, vt, u2, mask, carry):
    m = qs.shape[0]
    n = kt.shape[1] // LANE
    z = jnp.dot(qs, kt, preferred_element_type=F32)
    sp = _softplus(z)
    lk = -sp if mask is None else jnp.where(mask, -sp, 0.0)
    hi = lk.astype(BF16)
    lo = (lk - hi.astype(F32)).astype(BF16)
    stacked = jnp.concatenate([x[:, p * LANE:(p + 1) * LANE] for p in range(n) for x in (hi, lo)], axis=0)
    r = jnp.dot(stacked, u2, preferred_element_type=F32)
    tails = [None] * n
    run = carry
    for p in reversed(range(n)):
        rp = r[2 * p * m:(2 * p + 1) * m] + r[(2 * p + 1) * m:(2 * p + 2) * m]
        tails[p] = run + rp[:, :LANE]
        run = run + rp[:, LANE:]
    tail = tails[0] if n == 1 else jnp.concatenate(tails, axis=1)
    a = jnp.exp(z - sp + tail)
    if mask is not None:
        a = jnp.where(mask, a, 0.0)
    return _dot_nt(a.astype(BF16), vt), run


def _dec_mla(qm, lat, krt, mask, m_prev, l_prev, acc_prev):
    s = (_dot_nt(qm[:, :MLA_KV_LORA], lat) +
         jnp.dot(qm[:, MLA_KV_LORA:MLA_KV_LORA + MLA_ROPE], krt, preferred_element_type=F32)) * MLA_SCALE
    if mask is not None:
        s = jnp.where(mask, s, NEG_BIG)
    m_new = jnp.maximum(m_prev, jnp.max(s, axis=-1, keepdims=True))
    p = jnp.exp(s - m_new)
    alpha = jnp.exp(m_prev - m_new)
    l_new = alpha * l_prev + jnp.sum(p, axis=-1, keepdims=True)
    acc_new = alpha * acc_prev + jnp.dot(p.astype(BF16), lat, preferred_element_type=F32)
    return m_new, l_new, acc_new


def _dec_kernel(pt_ref, qsb_ref, qm_ref, kn_ref, vn_ref, ln_ref, rn_ref, u2_ref, wuv_ref, *rest, cpages, dec_seq):
    k_refs = rest[0:cpages]
    v_refs = rest[cpages:2 * cpages]
    l_refs = rest[2 * cpages:3 * cpages]
    r_refs = rest[3 * cpages:4 * cpages]
    oa_ref, oc_ref, sacc, scar, m_scr, l_scr, macc = rest[4 * cpages:]
    c = pl.program_id(1)
    nq = qsb_ref.shape[1]
    qs = (qsb_ref[0] * (HEAD_DIM ** -0.5)).astype(BF16)
    qm = qm_ref[0]
    u2 = u2_ref[...]

    def step(kt, vt, lat, krt, mask_sb, mask_mla):
        pv, car = _dec_sb(qs, kt, vt, u2, mask_sb, scar[...])
        sacc[...] += pv
        scar[...] = car
        m_new, l_new, acc_new = _dec_mla(qm, lat, krt, mask_mla, m_scr[...], l_scr[...], macc[...])
        m_scr[...] = m_new
        l_scr[...] = l_new
        macc[...] = acc_new

    @pl.when(c == 0)
    def _():
        sacc[...] = jnp.zeros_like(sacc)
        scar[...] = jnp.zeros_like(scar)
        m_scr[...] = jnp.full_like(m_scr, NEG_BIG)
        l_scr[...] = jnp.zeros_like(l_scr)
        macc[...] = jnp.zeros_like(macc)
        t = lax.broadcasted_iota(jnp.int32, (nq, LANE), 0) % dec_seq
        s = lax.broadcasted_iota(jnp.int32, (nq, LANE), 1)
        step(kn_ref[0].astype(BF16), vn_ref[0].astype(BF16), ln_ref[0].astype(BF16), rn_ref[0].astype(BF16),
             s < t, s <= t)

    kt = jnp.concatenate([k_refs[p][0, 0].astype(BF16) for p in range(cpages)], axis=1)
    vt = jnp.concatenate([v_refs[p][0, 0].astype(BF16) for p in range(cpages)], axis=1)
    lat = jnp.concatenate([l_refs[p][0, 0].astype(BF16) for p in range(cpages)], axis=0)
    krt = jnp.concatenate([r_refs[p][0, 0].astype(BF16) for p in range(cpages)], axis=1)
    step(kt, vt, lat, krt, None, None)

    @pl.when(c == pl.num_programs(1) - 1)
    def _():
        row = lax.broadcasted_iota(jnp.int32, (nq, LANE), 0)
        lane = lax.broadcasted_iota(jnp.int32, (nq, LANE), 1)
        kv_of_row = row // (dec_seq * SB_GROUP)
        oa_ref[0] = jnp.where((lane // HEAD_DIM) == kv_of_row, sacc[...], 0.0).astype(oa_ref.dtype)
        ol = (macc[...] / l_scr[...]).astype(BF16)
        oc = jnp.zeros((nq, LANE), F32)
        head_of_row = row // dec_seq
        for h in range(MLA_HEADS):
            oh = jnp.dot(ol, wuv_ref[h], preferred_element_type=F32)
            oc = oc + jnp.where(head_of_row == h, oh, 0.0)
        oc_ref[0] = oc.astype(oc_ref.dtype)


def _dec_attention(layer, page_table, q_sb, q_mla, k_new, v_new, lat_new, kr_new, cache_k, cache_v, cache_lat,
                   cache_kr, wuv, dec_seq):
    DB, NP = page_table.shape
    cp = min(DEC_PAGES_PER_STEP, NP)
    nc = NP // cp
    nq = q_sb.shape[1]
    page = cache_lat.shape[2]
    qw = q_mla.shape[2]

    def pmap(p):
        return lambda b, c, pt: (layer, pt[b * NP + (nc - 1 - c) * cp + p], 0, 0)

    bmap = lambda b, c, pt: (b, 0, 0)
    in_specs = [
        pl.BlockSpec((1, nq, LANE), bmap),
        pl.BlockSpec((1, nq, qw), bmap),
        pl.BlockSpec((1, LANE, LANE), bmap),
        pl.BlockSpec((1, LANE, LANE), bmap),
        pl.BlockSpec((1, LANE, MLA_KV_LORA), bmap),
        pl.BlockSpec((1, MLA_ROPE, LANE), bmap),
        pl.BlockSpec((LANE, 2 * LANE), lambda b, c, pt: (0, 0)),
        pl.BlockSpec((MLA_HEADS, MLA_KV_LORA, LANE), lambda b, c, pt: (0, 0, 0)),
    ]
    in_specs += [pl.BlockSpec((1, 1, LANE, page), pmap(p)) for p in range(cp)]
    in_specs += [pl.BlockSpec((1, 1, LANE, page), pmap(p)) for p in range(cp)]
    in_specs += [pl.BlockSpec((1, 1, page, MLA_KV_LORA), pmap(p)) for p in range(cp)]
    in_specs += [pl.BlockSpec((1, 1, MLA_ROPE, page), pmap(p)) for p in range(cp)]
    grid_spec = pltpu.PrefetchScalarGridSpec(
        num_scalar_prefetch=1,
        grid=(DB, nc),
        in_specs=in_specs,
        out_specs=(pl.BlockSpec((1, nq, LANE), bmap), pl.BlockSpec((1, nq, LANE), bmap)),
        scratch_shapes=[pltpu.VMEM((nq, LANE), F32), pltpu.VMEM((nq, LANE), F32), pltpu.VMEM((nq, 1), F32),
                        pltpu.VMEM((nq, 1), F32), pltpu.VMEM((nq, MLA_KV_LORA), F32)],
    )
    args = [page_table.reshape(-1), q_sb, q_mla, k_new, v_new, lat_new, kr_new, _sb_tri(LANE), wuv]
    args += [cache_k] * cp + [cache_v] * cp + [cache_lat] * cp + [cache_kr] * cp
    return pl.pallas_call(
        functools.partial(_dec_kernel, cpages=cp, dec_seq=dec_seq),
        out_shape=(jax.ShapeDtypeStruct((DB, nq, LANE), BF16), jax.ShapeDtypeStruct((DB, nq, LANE), BF16)),
        grid_spec=grid_spec,
        compiler_params=_cparams(("parallel", "arbitrary")),
        name="dec_attention",
    )(*args)


def _merge_kernel(oa_ref, ob_ref, oc_ref, od_ref, g_ref, wa_ref, wb_ref, wc_ref, wd_ref, o_ref):
    acc = None
    for i, (o, w) in enumerate(((oa_ref, wa_ref), (ob_ref, wb_ref), (oc_ref, wc_ref), (od_ref, wd_ref))):
        p = jnp.dot(o[...], w[...], preferred_element_type=F32)
        g = g_ref[:, i * D_MODEL:(i + 1) * D_MODEL].astype(F32)
        acc = g * p if acc is None else acc + g * p
    o_ref[...] = acc.astype(o_ref.dtype)


def _merge(oa, ob, oc, od, gates, wa, wb, wc, wd):
    M = oa.shape[0]
    tm = min(256, M)
    row = lambda i: (i, 0)
    const = lambda i: (0, 0)
    return pl.pallas_call(
        _merge_kernel,
        out_shape=jax.ShapeDtypeStruct((M, D_MODEL), BF16),
        grid=(M // tm,),
        in_specs=[pl.BlockSpec((tm, oa.shape[1]), row), pl.BlockSpec((tm, ob.shape[1]), row),
                  pl.BlockSpec((tm, oc.shape[1]), row), pl.BlockSpec((tm, od.shape[1]), row),
                  pl.BlockSpec((tm, gates.shape[1]), row),
                  pl.BlockSpec(wa.shape, const), pl.BlockSpec(wb.shape, const),
                  pl.BlockSpec(wc.shape, const), pl.BlockSpec(wd.shape, const)],
        out_specs=pl.BlockSpec((tm, D_MODEL), row),
        compiler_params=_cparams(("parallel",)),
        name="merge",
    )(oa, ob, oc, od, gates, wa, wb, wc, wd)


def _out_kernel(mg_ref, x_ref, g1_ref, sc2_ref, sh2_ref, w_ref, lg_ref, lb_ref, x1_ref, h2_ref):
    mg = mg_ref[...]
    mix = jnp.dot(mg.reshape(-1, D_MODEL), w_ref[...], preferred_element_type=F32).reshape(x_ref.shape)
    x1 = _ln(DEEPNORM_ALPHA * x_ref[...] + g1_ref[...] * mix) * lg_ref[...] + lb_ref[...]
    x1_ref[...] = x1
    h2_ref[...] = (_ln(x1) * (1.0 + sc2_ref[...]) + sh2_ref[...]).astype(h2_ref.dtype)


def _out_proj(merged3, x3, mod, w_out, ln_g, ln_b):
    G, R, D = x3.shape
    gb, tr = _row_tiling(x3, mod)
    if mod.shape[1] == 1:
        tr = min(256, tr)
    blk = pl.BlockSpec((gb, tr, D), lambda g, r: (g, r, 0))
    vec = pl.BlockSpec((1, 1, D), lambda g, r: (0, 0, 0))
    return pl.pallas_call(
        _out_kernel,
        out_shape=(jax.ShapeDtypeStruct((G, R, D), F32), jax.ShapeDtypeStruct((G, R, D), BF16)),
        grid=(G // gb, R // tr),
        in_specs=[blk, blk, _mod_spec(mod, 2, tr, 2), _mod_spec(mod, 4, tr, 2), _mod_spec(mod, 3, tr, 2),
                  pl.BlockSpec((D, D), lambda g, r: (0, 0)), vec, vec],
        out_specs=(blk, blk),
        compiler_params=_cparams(("parallel", "parallel")),
        name="out_proj",
    )(merged3, x3, mod, mod, mod, w_out, ln_g.reshape(1, 1, D), ln_b.reshape(1, 1, D))


def _route(h, wrg_ref, brg_ref, wre_ref, bre_ref):
    gl = jnp.dot(h, wrg_ref[...], preferred_element_type=F32) + brg_ref[...]
    el = jnp.dot(h, wre_ref[...], preferred_element_type=F32) + bre_ref[...]
    lane_i = lax.broadcasted_iota(jnp.int32, gl.shape, 1)
    lane = lane_i.astype(F32)
    big = float(1 << 20)
    gl = jnp.where(lane_i < N_GROUPS, gl, NEG_BIG)
    gmax = jnp.max(gl, axis=-1, keepdims=True)
    gidx = jnp.min(jnp.where(gl == gmax, lane, big), axis=-1, keepdims=True)
    gw = 1.0 / jnp.sum(jnp.exp(gl - gmax), axis=-1, keepdims=True)
    valid = ((lane_i // EXPERTS_PER_GROUP).astype(F32) == gidx) & (lane_i < N_EXPERTS)
    e1 = jnp.where(valid, el, NEG_BIG)
    l1 = jnp.max(e1, axis=-1, keepdims=True)
    i1 = jnp.min(jnp.where(e1 == l1, lane, big), axis=-1, keepdims=True)
    e2 = jnp.where(lane == i1, NEG_BIG, e1)
    l2 = jnp.max(e2, axis=-1, keepdims=True)
    i2 = jnp.min(jnp.where(e2 == l2, lane, big), axis=-1, keepdims=True)
    t = jnp.exp(l2 - l1)
    w1 = 1.0 / (1.0 + t)
    w2 = t / (1.0 + t)
    return gw * (jnp.where(lane == i1, w1, 0.0) + jnp.where(lane == i2, w2, 0.0))


def _moe_kernel(h_ref, x_ref, g2_ref, wrg_ref, brg_ref, wre_ref, bre_ref, wgu_ref, wd_ref, lg_ref, lb_ref,
                o_ref, comb_scr, acc_scr):
    e = pl.program_id(2)
    h = h_ref[...].reshape(-1, D_MODEL)

    @pl.when(e == 0)
    def _():
        comb_scr[...] = _route(h, wrg_ref, brg_ref, wre_ref, bre_ref)
        acc_scr[...] = jnp.zeros_like(acc_scr)

    comb = comb_scr[...]
    lane = lax.broadcasted_iota(jnp.int32, comb.shape, 1)
    ce = jnp.sum(jnp.where(lane == e, comb, 0.0), axis=-1, keepdims=True)
    au = jnp.dot(h, wgu_ref[0], preferred_element_type=F32)
    a = au[:, :EXPERT_FF]
    u = au[:, EXPERT_FF:]
    hid = (a * jax.nn.sigmoid(a)) * u * ce
    acc_scr[...] += jnp.dot(hid.astype(BF16), wd_ref[0], preferred_element_type=F32)

    @pl.when(e == pl.num_programs(2) - 1)
    def _():
        moe = acc_scr[...].reshape(x_ref.shape)
        o_ref[...] = _ln(DEEPNORM_ALPHA * x_ref[...] + g2_ref[...] * moe) * lg_ref[...] + lb_ref[...]


def _moe(h3, x3, mod, wrg, brg, wre, bre, wgu, wd, ln_g, ln_b):
    G, R, D = x3.shape
    gb, tr = _row_tiling(x3, mod)
    E = wgu.shape[0]
    blk = pl.BlockSpec((gb, tr, D), lambda g, r, e: (g, r, 0))
    vec = pl.BlockSpec((1, 1, D), lambda g, r, e: (0, 0, 0))
    c2 = lambda g, r, e: (0, 0)
    gm, rm, _ = mod.shape
    if rm == 1:
        g2_spec = pl.BlockSpec((1, 1, D), lambda g, r, e: (g, 0, 5))
    else:
        g2_spec = pl.BlockSpec((1, tr, D), lambda g, r, e: (0, r, 5))
    return pl.pallas_call(
        _moe_kernel,
        out_shape=jax.ShapeDtypeStruct((G, R, D), F32),
        grid=(G // gb, R // tr, E),
        in_specs=[blk, blk, g2_spec,
                  pl.BlockSpec((D, LANE), c2), pl.BlockSpec((1, LANE), c2),
                  pl.BlockSpec((D, LANE), c2), pl.BlockSpec((1, LANE), c2),
                  pl.BlockSpec((1, D, 2 * EXPERT_FF), lambda g, r, e: (e, 0, 0)),
                  pl.BlockSpec((1, EXPERT_FF, D), lambda g, r, e: (e, 0, 0)),
                  vec, vec],
        out_specs=blk,
        scratch_shapes=[pltpu.VMEM((gb * tr, LANE), F32), pltpu.VMEM((gb * tr, D), F32)],
        compiler_params=_cparams(("parallel", "parallel", "arbitrary")),
        name="moe",
    )(h3, x3, mod, wrg, brg, wre, bre, wgu, wd, ln_g.reshape(1, 1, D), ln_b.reshape(1, 1, D))


def _head_slots(w, n_heads, width, lane_off):
    lead = w.shape[:-1]
    parts = []
    for h in range(n_heads):
        piece = w[..., h * width:(h + 1) * width]
        parts.append(jnp.pad(piece, [(0, 0)] * len(lead) + [(lane_off[h], LANE - width - lane_off[h])]))
    return jnp.concatenate(parts, axis=-1)


def _prep_w_in(w_in):
    wb = w_in.astype(BF16)
    o = 0
    pieces = {}
    for name, wd in (("q", 512), ("k", 128), ("v", 128), ("glu", 1024), ("cq", 512), ("ckv", 256), ("kr", 32),
                     ("pool", 512), ("gates", 4 * D_MODEL)):
        pieces[name] = wb[..., o:o + wd]
        o += wd
    q_off = [0 if h < SB_GROUP else HEAD_DIM for h in range(SB_HEADS)]
    kv_off = [0, HEAD_DIM]
    kr = pieces["kr"]
    half = MLA_ROPE // 2
    kr_sw = jnp.concatenate([kr[..., half:], kr[..., :half]], axis=-1)
    small = jnp.concatenate([
        _head_slots(pieces["q"], SB_HEADS, HEAD_DIM, q_off),
        _head_slots(pieces["k"], SB_KV_HEADS, HEAD_DIM, kv_off),
        _head_slots(pieces["v"], SB_KV_HEADS, HEAD_DIM, kv_off),
        pieces["glu"], pieces["cq"], pieces["ckv"],
        _head_slots(kr, 1, MLA_ROPE, [0]), _head_slots(kr_sw, 1, MLA_ROPE, [0]),
        pieces["pool"]], axis=-1)
    return small, pieces["gates"]


def _prep_branch(w_branch):
    wb = w_branch.astype(BF16)
    q_off = [0 if h < SB_GROUP else HEAD_DIM for h in range(SB_HEADS)]
    wa = jnp.swapaxes(_head_slots(jnp.swapaxes(wb[:, 0], 1, 2), SB_HEADS, HEAD_DIM, q_off), 1, 2)
    wc = jnp.swapaxes(_head_slots(jnp.swapaxes(wb[:, 2], 1, 2), MLA_HEADS, HEAD_DIM, [0] * MLA_HEADS), 1, 2)
    return wa, wb[:, 1], wc, wb[:, 3]


def _prep_mla(w_uq, w_uk, w_uv):
    L = w_uq.shape[0]
    wq = w_uq.astype(BF16)
    half = MLA_ROPE // 2
    nope = wq[..., :MLA_NOPE].reshape(L, MLA_Q_LORA, MLA_HEADS * MLA_NOPE)
    rp = wq[..., MLA_NOPE:]
    rp_sw = jnp.concatenate([rp[..., half:], rp[..., :half]], axis=-1)
    rp = rp.reshape(L, MLA_Q_LORA, MLA_HEADS * MLA_ROPE)
    rp_sw = rp_sw.reshape(L, MLA_Q_LORA, MLA_HEADS * MLA_ROPE)
    z8 = [0] * MLA_HEADS
    wq3 = jnp.concatenate([_head_slots(nope, MLA_HEADS, MLA_NOPE, z8), _head_slots(rp, MLA_HEADS, MLA_ROPE, z8),
                           _head_slots(rp_sw, MLA_HEADS, MLA_ROPE, z8)], axis=-1)
    wukT = jnp.transpose(w_uk.astype(BF16), (0, 2, 3, 1))
    wukT = jnp.pad(wukT, ((0, 0), (0, 0), (0, LANE - MLA_NOPE), (0, 0)))
    wuv = jnp.transpose(w_uv.astype(BF16), (0, 2, 1, 3))
    wuv = jnp.pad(wuv, ((0, 0), (0, 0), (0, 0), (0, LANE - HEAD_DIM)))
    return wq3, wukT, wuv


def _rope_tables(pos):
    half = MLA_ROPE // 2
    inv_freq = ROPE_THETA ** (-jnp.arange(half, dtype=F32) / half)
    ang = pos.astype(F32)[:, None] * inv_freq
    cos, sin = jnp.cos(ang), jnp.sin(ang)
    pad = ((0, 0), (0, LANE - MLA_ROPE))
    return jnp.pad(jnp.concatenate([cos, cos], -1), pad), jnp.pad(jnp.concatenate([-sin, sin], -1), pad)


def _unslot(z, base, n_heads, lane_off):
    return jnp.stack([z[..., base + h * LANE + lane_off[h]:base + h * LANE + lane_off[h] + HEAD_DIM]
                      for h in range(n_heads)], axis=-2)


def kernel(x_prompt, x_sample, cache_sb_k, cache_sb_v, cache_mla_latent, cache_mla_krope, state_conv, state_pool, page_table, c_prompt, c_sample, w_ada, b_ada, w_in, conv_w, conv_b, conv_ln_g, conv_ln_b, mla_q_norm, w_uq, mla_kv_norm, w_uk, w_uv, pool_w, pool_scale, w_branch, w_out, ln1_g, ln1_b, w_router_grp, b_router_grp, w_router_exp, b_router_exp, w_exp_gate, w_exp_up, w_exp_down, ln2_g, ln2_b):
    B, T, D = x_prompt.shape
    DB, TS, _ = x_sample.shape
    L = w_ada.shape[0]
    n_pool, page = cache_sb_k.shape[1], cache_sb_k.shape[2]
    past_len = page_table.shape[1] * page
    kv_off = [0, HEAD_DIM]

    w_small, w_gates = _prep_w_in(w_in)
    wa, wb_, wc, wd_ = _prep_branch(w_branch)
    wq3, wukT, wuv = _prep_mla(w_uq, w_uk, w_uv)
    w_out_b = w_out.astype(BF16)
    pool_w_b = pool_w.astype(BF16)
    wrg = jnp.pad(w_router_grp.astype(BF16), ((0, 0), (0, 0), (0, LANE - N_GROUPS)))
    brg = jnp.pad(b_router_grp, ((0, 0), (0, LANE - N_GROUPS)))
    wre = jnp.pad(w_router_exp.astype(BF16).reshape(L, D, N_EXPERTS), ((0, 0), (0, 0), (0, LANE - N_EXPERTS)))
    bre = jnp.pad(b_router_exp.reshape(L, N_EXPERTS), ((0, 0), (0, LANE - N_EXPERTS)))
    wgu = jnp.concatenate([w_exp_gate.astype(BF16), w_exp_up.astype(BF16)], axis=-1)
    wdn = w_exp_down.astype(BF16)
    cache_kt = jnp.transpose(cache_sb_k, (0, 1, 3, 4, 2)).reshape(L, n_pool, SB_KV_HEADS * HEAD_DIM, page)
    cache_vt = jnp.transpose(cache_sb_v, (0, 1, 3, 4, 2)).reshape(L, n_pool, SB_KV_HEADS * HEAD_DIM, page)
    cache_krt = jnp.swapaxes(cache_mla_krope, 2, 3)

    cos_p, sin_p = _rope_tables(jnp.arange(T))
    cos_p, sin_p = jnp.tile(cos_p, (B, 1)), jnp.tile(sin_p, (B, 1))
    cos_s, sin_s = _rope_tables(past_len + jnp.arange(TS))
    cos_s, sin_s = jnp.repeat(cos_s, DB, axis=0), jnp.repeat(sin_s, DB, axis=0)

    n_c = B + DB
    n_c_pad = -(-n_c // 8) * 8
    c_all = jnp.pad(jnp.concatenate([c_prompt, c_sample], axis=0), ((0, n_c_pad - n_c), (0, 0)))
    mod_all = _ada(c_all, w_ada, b_ada)

    xp = x_prompt
    xs = jnp.swapaxes(x_sample, 0, 1)
    rows_p = [[] for _ in range(6)]
    rows_s = [[] for _ in range(6)]
    for l in range(L):
        mod_p = mod_all[l, :B][:, None, :]
        mod_s = mod_all[l, B:B + DB][None]

        zp = _lnmm(xp, mod_p, 1, 0, w_small[l], None, F32, 512)
        gp = _lnmm(xp, mod_p, 1, 0, w_gates[l], "sigmoid", BF16, 1024)
        oa_p = _sb_prompt(zp)
        qcat_p, lat_p, kro_p = _mla_prep(zp.reshape(B * T, Z_SMALL), cos_p, sin_p, mla_q_norm[l][None],
                                         mla_kv_norm[l][None], wq3[l], wukT[l])
        oc_p = _mla_prompt(qcat_p.reshape(B, T, -1), lat_p.reshape(B, T, -1), kro_p.reshape(B, T, -1), wuv[l])
        ob_p, od_p, tail_p = _mix_prompt(zp, conv_w[l], conv_b[l][None], conv_ln_g[l][None], conv_ln_b[l][None],
                                         pool_w_b[l], pool_scale[l][None])
        mg_p = _merge(oa_p.reshape(B * T, -1), ob_p.reshape(B * T, -1), oc_p.reshape(B * T, -1),
                      od_p.reshape(B * T, -1), gp.reshape(B * T, -1), wa[l], wb_[l], wc[l], wd_[l])
        x1_p, h2_p = _out_proj(mg_p.reshape(B, T, D), xp, mod_p, w_out_b[l], ln1_g[l], ln1_b[l])
        xp = _moe(h2_p, x1_p, mod_p, wrg[l], brg[l][None], wre[l], bre[l][None], wgu[l], wdn[l], ln2_g[l], ln2_b[l])

        rows_p[0].append(_unslot(zp, ZC_K, SB_KV_HEADS, kv_off))
        rows_p[1].append(_unslot(zp, ZC_V, SB_KV_HEADS, kv_off))
        rows_p[2].append(lat_p.reshape(B, T, MLA_KV_LORA))
        rows_p[3].append(kro_p.reshape(B, T, LANE)[..., :MLA_ROPE])
        rows_p[4].append(tail_p[:, HALO - (CONV_WIDTH - 1):])
        rows_p[5].append(zp[:, T - POOL_STATE:, ZC_POOL:ZC_POOL + BRANCH_W])

        zs = _lnmm(xs, mod_s, 1, 0, w_small[l], None, F32, 512)
        gs = _lnmm(xs, mod_s, 1, 0, w_gates[l], "sigmoid", BF16, 1024)
        zs2 = zs.reshape(TS * DB, Z_SMALL)
        qcat_s, lat_s, kro_s = _mla_prep(zs2, cos_s, sin_s, mla_q_norm[l][None], mla_kv_norm[l][None], wq3[l], wukT[l])
        conv_tm = jnp.swapaxes(state_conv[l], 0, 1).reshape((CONV_WIDTH - 1) * DB, BRANCH_W)
        pool_tm = jnp.swapaxes(state_pool[l], 0, 1).reshape(POOL_STATE * DB, BRANCH_W)
        ob_s, od_s, u_s = _mix_sample(zs2, conv_tm, pool_tm, DB, past_len, conv_w[l], conv_b[l][None],
                                      conv_ln_g[l][None], conv_ln_b[l][None], pool_w_b[l], pool_scale[l][None])

        def bm(a, heads):
            w_ = a.shape[-1] // heads
            return jnp.transpose(a.reshape(TS, DB, heads, w_), (1, 2, 0, 3)).reshape(DB, heads * TS, w_)

        def pad_keys(a):
            a = jnp.swapaxes(a, 0, 1)
            return jnp.pad(a, ((0, 0), (0, LANE - TS), (0, 0)))

        def pad_keys_t(a):
            a = jnp.transpose(a, (1, 2, 0))
            return jnp.pad(a, ((0, 0), (0, 0), (0, LANE - TS)))

        k_new = pad_keys_t(zs[..., ZC_K:ZC_K + LANE] + zs[..., ZC_K + LANE:ZC_K + 2 * LANE])
        v_new = pad_keys_t(zs[..., ZC_V:ZC_V + LANE] + zs[..., ZC_V + LANE:ZC_V + 2 * LANE])
        lat_new = pad_keys(lat_s.reshape(TS, DB, MLA_KV_LORA))
        kr_new = pad_keys_t(kro_s.reshape(TS, DB, LANE)[..., :MLA_ROPE])
        oa_b, oc_b = _dec_attention(l, page_table, bm(zs2[:, ZC_Q:ZC_Q + SB_HEADS * LANE], SB_HEADS),
                                    bm(qcat_s, MLA_HEADS), k_new, v_new, lat_new, kr_new,
                                    cache_kt, cache_vt, cache_mla_latent, cache_krt, wuv[l], TS)

        def tm(a, heads):
            return jnp.transpose(a.reshape(DB, heads, TS, LANE), (2, 0, 1, 3)).reshape(TS * DB, heads * LANE)

        mg_s = _merge(tm(oa_b, SB_HEADS), ob_s, tm(oc_b, MLA_HEADS), od_s, gs.reshape(TS * DB, -1),
                      wa[l], wb_[l], wc[l], wd_[l])
        x1_s, h2_s = _out_proj(mg_s.reshape(TS, DB, D), xs, mod_s, w_out_b[l], ln1_g[l], ln1_b[l])
        xs = _moe(h2_s, x1_s, mod_s, wrg[l], brg[l][None], wre[l], bre[l][None], wgu[l], wdn[l], ln2_g[l], ln2_b[l])

        zs_b = jnp.swapaxes(zs, 0, 1)
        rows_s[0].append(_unslot(zs_b, ZC_K, SB_KV_HEADS, kv_off))
        rows_s[1].append(_unslot(zs_b, ZC_V, SB_KV_HEADS, kv_off))
        rows_s[2].append(jnp.swapaxes(lat_s.reshape(TS, DB, MLA_KV_LORA), 0, 1))
        rows_s[3].append(jnp.swapaxes(kro_s.reshape(TS, DB, LANE)[..., :MLA_ROPE], 0, 1))
        u_b = jnp.swapaxes(u_s.reshape(TS, DB, BRANCH_W), 0, 1)
        rows_s[4].append(jnp.concatenate([state_conv[l], u_b], axis=1)[:, -(CONV_WIDTH - 1):])
        rows_s[5].append(jnp.concatenate([state_pool[l], zs_b[..., ZC_POOL:ZC_POOL + BRANCH_W]], axis=1)[:, -POOL_STATE:])

    outs_p = [jnp.stack(r) for r in rows_p]
    outs_s = [jnp.stack(r) for r in rows_s]
    return (xp, jnp.swapaxes(xs, 0, 1), *outs_p, *outs_s)
```
